```python
import math
import jax, jax.numpy as jnp
from jax import lax
import numpy as np

D_MODEL = 1024
BATCH = 8
SEQ = 8192
DEPTH = 2

N_META = 16
BLOCK = 128
MIX_WIDTH = D_MODEL
N_RET_HEADS = 4
RET_DIM = MIX_WIDTH // 2 // N_RET_HEADS
N_DIFF_HEADS = 4
DIFF_V_DIM = MIX_WIDTH // 2 // N_DIFF_HEADS
DIFF_QK_DIM = DIFF_V_DIM // 2
N_SB_HEADS = 8
SB_DIM = MIX_WIDTH // N_SB_HEADS
D_FF = ((8 * D_MODEL // 3 + 127) // 128) * 128
CONV_WIDTH = 3
EPS = 1e-6
MASK_VALUE = -1e30
N_EVEN = (DEPTH + 1) // 2
N_ODD = DEPTH // 2
RET_W = N_RET_HEADS * RET_DIM
DIFF_QK_W = N_DIFF_HEADS * 2 * DIFF_QK_DIM
DIFF_V_W = N_DIFF_HEADS * DIFF_V_DIM
AB_IN = 4 * RET_W + 2 * DIFF_QK_W + DIFF_V_W
C_IN = 3 * N_SB_HEADS * SB_DIM

kernel_name = "hybrid_retention_diffattn_stickbreaking_convffn"


def rmsnorm(x, g):
    xf = x.astype(jnp.float32)
    y = xf * lax.rsqrt(jnp.mean(xf * xf, axis=-1, keepdims=True) + EPS)
    return y.astype(x.dtype) * g


def head_rmsnorm(x, g):
    xf = x.astype(jnp.float32)
    y = xf * lax.rsqrt(jnp.mean(xf * xf, axis=-1, keepdims=True) + EPS)
    return y.astype(x.dtype) * g.reshape(x.shape[-2:])


def head_groupnorm(x, g):
    xf = x.astype(jnp.float32)
    mu = jnp.mean(xf, axis=-1, keepdims=True)
    xc = xf - mu
    y = xc * lax.rsqrt(jnp.mean(xc * xc, axis=-1, keepdims=True) + EPS)
    return y.astype(x.dtype) * g.reshape(x.shape[-2:])


def to_blocks(t):
    b, p, h, d = t.shape
    return t.reshape(b, p // BLOCK, BLOCK, h, d).transpose(1, 0, 3, 2, 4)


def from_blocks(t):
    n, b, h, c, d = t.shape
    return t.transpose(1, 0, 3, 2, 4).reshape(b, n * c, h, d)


def retention(q, k, v, valid):
    b, p, h, dk = q.shape
    dv = v.shape[-1]
    k = jnp.where(valid[None, :, None, None], k * dk ** -0.5, 0)
    v = jnp.where(valid[None, :, None, None], v, 0)
    qc, kc, vc = to_blocks(q), to_blocks(k), to_blocks(v)
    log_g = jnp.log1p(-(2.0 ** (-5.0 - jnp.arange(h, dtype=jnp.float32))))
    j = jnp.arange(BLOCK, dtype=jnp.float32)
    rel = j[:, None] - j[None, :]
    decay = jnp.where(rel >= 0, jnp.exp(log_g[:, None, None] * jnp.maximum(rel, 0.0)), 0.0).astype(q.dtype)
    q_decay = jnp.exp(log_g[:, None] * (j + 1.0))[:, :, None].astype(q.dtype)
    k_decay = jnp.exp(log_g[:, None] * (BLOCK - 1.0 - j))[:, :, None].astype(q.dtype)
    chunk_decay = jnp.exp(log_g * BLOCK)[:, None, None].astype(q.dtype)
    scores = jnp.einsum('nbhqd,nbhkd->nbhqk', qc, kc) * decay
    inner = jnp.einsum('nbhqk,nbhke->nbhqe', scores, vc)
    kv = jnp.einsum('nbhkd,nbhke->nbhde', kc * k_decay, vc)

    def step(state, kv_n):
        return chunk_decay * state + kv_n, state

    _, prev = lax.scan(step, jnp.zeros((b, h, dk, dv), kv.dtype), kv)
    cross = jnp.einsum('nbhqd,nbhde->nbhqe', qc * q_decay, prev)
    return from_blocks(inner + cross)


def diff_attention(q1, q2, k1, k2, v, lam, valid):
    b, p, h, d = q1.shape
    n = p // BLOCK
    scale = d ** -0.5
    slopes = 2.0 ** (-8.0 * (jnp.arange(h, dtype=jnp.float32) + 1.0) / h)
    k1t, k2t, vt = (t.transpose(0, 2, 1, 3) for t in (k1, k2, v))
    key_pos = jnp.arange(p)

    def block(args):
        i, qb1, qb2 = args
        qpos = i * BLOCK + jnp.arange(BLOCK)
        dist = qpos[:, None] - key_pos[None, :]
        mask = (dist >= 0) & valid[None, :]
        bias = -slopes[:, None, None] * dist.astype(jnp.float32)

        def probs(qb, kt):
            s = jnp.einsum('bhqd,bhkd->bhqk', qb, kt).astype(jnp.float32) * scale + bias
            return jax.nn.softmax(jnp.where(mask, s, MASK_VALUE), axis=-1)

        a = probs(qb1, k1t) - lam * probs(qb2, k2t)
        return jnp.einsum('bhqk,bhke->bhqe', a.astype(vt.dtype), vt)

    out = lax.map(block, (jnp.arange(n), to_blocks(q1), to_blocks(q2)))
    return from_blocks(out)


def stick_breaking(q, k, v, valid):
    b, p, h, d = q.shape
    n = p // BLOCK
    scale = d ** -0.5
    kt, vt = k.transpose(0, 2, 1, 3), v.transpose(0, 2, 1, 3)
    key_pos = jnp.arange(p)

    def block(args):
        i, qb = args
        qpos = i * BLOCK + jnp.arange(BLOCK)
        mask = (qpos[:, None] > key_pos[None, :]) & valid[None, :]
        z = jnp.einsum('bhqd,bhkd->bhqk', qb, kt).astype(jnp.float32) * scale
        log_1m = jnp.where(mask, -jax.nn.softplus(z), 0.0)
        later = lax.cumsum(log_1m, axis=3, reverse=True) - log_1m
        w = jnp.where(mask, jnp.exp(jax.nn.log_sigmoid(z) + later), 0.0)
        return jnp.einsum('bhqk,bhke->bhqe', w.astype(vt.dtype), vt)

    out = lax.map(block, (jnp.arange(n), to_blocks(q)))
    return from_blocks(out)


def conv_ffn(h, w_up, w_conv, b_conv, w_down, valid):
    p = h.shape[1]
    gate, val = jnp.split(h @ w_up, 2, axis=-1)
    gate = jnp.where(valid[None, :, None], gate, 0)
    gp = jnp.pad(gate, ((0, 0), (CONV_WIDTH - 1, 0), (0, 0)))
    conv = b_conv + sum(gp[:, tap:tap + p] * w_conv[tap] for tap in range(CONV_WIDTH))
    return (jax.nn.silu(conv) * val) @ w_down


def mixer_ab(h, w_in, ret_norm, diff_norm, lam_q1, lam_k1, lam_q2, lam_k2, w_out, lambda_init, valid):
    b, p, _ = h.shape
    proj = h @ w_in
    cuts = [RET_W, 2 * RET_W, 3 * RET_W, 4 * RET_W, 4 * RET_W + DIFF_QK_W, 4 * RET_W + 2 * DIFF_QK_W]
    rq, rk, rv, rg, dq, dk, dv = jnp.split(proj, cuts, axis=-1)
    ret = retention(rq.reshape(b, p, N_RET_HEADS, RET_DIM), rk.reshape(b, p, N_RET_HEADS, RET_DIM),
                    rv.reshape(b, p, N_RET_HEADS, RET_DIM), valid)
    ret = head_groupnorm(ret, ret_norm).reshape(b, p, RET_W) * jax.nn.silu(rg)
    dq = dq.reshape(b, p, N_DIFF_HEADS, 2, DIFF_QK_DIM)
    dk = dk.reshape(b, p, N_DIFF_HEADS, 2, DIFF_QK_DIM)
    f32 = jnp.float32
    lam = (jnp.exp(jnp.sum(lam_q1.astype(f32) * lam_k1.astype(f32)))
           - jnp.exp(jnp.sum(lam_q2.astype(f32) * lam_k2.astype(f32))) + lambda_init)
    dif = diff_attention(dq[..., 0, :], dq[..., 1, :], dk[..., 0, :], dk[..., 1, :],
                         dv.reshape(b, p, N_DIFF_HEADS, DIFF_V_DIM), lam, valid)
    dif = (head_rmsnorm(dif, diff_norm) * (1.0 - lambda_init)).reshape(b, p, DIFF_V_W)
    return jnp.concatenate([ret, dif], axis=-1) @ w_out


def mixer_c(h, w_in, w_out, valid):
    b, p, _ = h.shape
    q, k, v = jnp.split(h @ w_in, 3, axis=-1)
    shp = (b, p, N_SB_HEADS, SB_DIM)
    o = stick_breaking(q.reshape(shp), k.reshape(shp), v.reshape(shp), valid)
    return o.reshape(b, p, MIX_WIDTH) @ w_out


def setup_inputs(seed: int = 0) -> dict:
    key = jax.random.key(seed)
    ks = jax.random.split(key, 20)

    def nrm(k, shape, scale):
        return jax.random.normal(k, shape, jnp.float32) * scale

    return {
        "x": nrm(ks[0], (BATCH, SEQ, D_MODEL), 1.0),
        "meta_tokens": nrm(ks[1], (N_META, D_MODEL), 1.0),
        "mix_norm": 1.0 + nrm(ks[2], (DEPTH, D_MODEL), 0.02),
        "ffn_norm": 1.0 + nrm(ks[3], (DEPTH, D_MODEL), 0.02),
        "ffn_up": nrm(ks[4], (DEPTH, D_MODEL, 2 * D_FF), D_MODEL ** -0.5),
        "ffn_conv": nrm(ks[5], (DEPTH, CONV_WIDTH, D_FF), CONV_WIDTH ** -0.5),
        "ffn_conv_b": nrm(ks[6], (DEPTH, D_FF), 0.01),
        "ffn_down": nrm(ks[7], (DEPTH, D_FF, D_MODEL), D_FF ** -0.5),
        "ab_w_in": nrm(ks[8], (N_EVEN, D_MODEL, AB_IN), D_MODEL ** -0.5),
        "ab_ret_norm": 1.0 + nrm(ks[9], (N_EVEN, RET_W), 0.02),
        "ab_diff_norm": 1.0 + nrm(ks[10], (N_EVEN, DIFF_V_W), 0.02),
        "ab_lam_q1": nrm(ks[11], (N_EVEN, DIFF_QK_DIM), 0.1),
        "ab_lam_k1": nrm(ks[12], (N_EVEN, DIFF_QK_DIM), 0.1),
        "ab_lam_q2": nrm(ks[13], (N_EVEN, DIFF_QK_DIM), 0.1),
        "ab_lam_k2": nrm(ks[14], (N_EVEN, DIFF_QK_DIM), 0.1),
        "ab_w_out": nrm(ks[15], (N_EVEN, MIX_WIDTH, D_MODEL), MIX_WIDTH ** -0.5),
        "c_w_in": nrm(ks[16], (N_ODD, D_MODEL, C_IN), D_MODEL ** -0.5),
        "c_w_out": nrm(ks[17], (N_ODD, MIX_WIDTH, D_MODEL), MIX_WIDTH ** -0.5),
        "final_norm": 1.0 + nrm(ks[18], (D_MODEL,), 0.02),
    }


def reference(x, meta_tokens, mix_norm, ffn_norm, ffn_up, ffn_conv, ffn_conv_b, ffn_down,
              ab_w_in, ab_ret_norm, ab_diff_norm, ab_lam_q1, ab_lam_k1, ab_lam_q2, ab_lam_k2, ab_w_out,
              c_w_in, c_w_out, final_norm):
    b = x.shape[0]
    pad = jnp.zeros((b, BLOCK - N_META, D_MODEL), x.dtype)
    meta = jnp.broadcast_to(meta_tokens[None].astype(x.dtype), (b, N_META, D_MODEL))
    h = jnp.concatenate([pad, meta, x], axis=1)
    p = h.shape[1]
    valid = jnp.arange(p) >= (BLOCK - N_META)
    for i in range(DEPTH):
        y = rmsnorm(h, mix_norm[i])
        if i % 2 == 0:
            e = i // 2
            lambda_init = 0.8 - 0.6 * math.exp(-0.3 * i)
            h = h + mixer_ab(y, ab_w_in[e], ab_ret_norm[e], ab_diff_norm[e], ab_lam_q1[e], ab_lam_k1[e],
                             ab_lam_q2[e], ab_lam_k2[e], ab_w_out[e], lambda_init, valid)
        else:
            o = i // 2
            h = h + mixer_c(y, c_w_in[o], c_w_out[o], valid)
        h = h + conv_ffn(rmsnorm(h, ffn_norm[i]), ffn_up[i], ffn_conv[i], ffn_conv_b[i], ffn_down[i], valid)
    return rmsnorm(h, final_norm)[:, BLOCK:, :]
```

```python
import functools
import math

import numpy as np
import jax
import jax.numpy as jnp
from jax import lax
from jax.experimental import pallas as pl
from jax.experimental.pallas import tpu as pltpu

F32 = jnp.float32
BF16 = jnp.bfloat16

N_META = 16
BLOCK = 128
N_PAD = BLOCK - N_META
EPS = 1e-6
MASK_VALUE = -1e30
HEAD_W = 128
N_RET_HEADS = 4
N_DIFF_HEADS = 4
DIFF_QK_DIM = 64
N_SB_HEADS = 8
CONV_WIDTH = 3
HALO = 16

VMEM_LIMIT_BYTES = 56 * 1024 * 1024


def _pick_tile(n, candidates):
    for c in candidates:
        if n % c == 0:
            return c
    raise ValueError(f"no tile for {n} in {candidates}")


def _params(sem):
    return pltpu.CompilerParams(dimension_semantics=sem, vmem_limit_bytes=VMEM_LIMIT_BYTES)


def _const_spec(shape):
    return pl.BlockSpec(shape, lambda *_: (0,) * len(shape), pipeline_mode=pl.Buffered(1))


def _rms(x, g):
    return x * lax.rsqrt(jnp.mean(x * x, axis=-1, keepdims=True) + EPS) * g


def _dot(a, b):
    return jnp.dot(a, b, preferred_element_type=F32)


def _dot_nt(a, b):
    return lax.dot_general(a, b, (((1,), (1,)), ((), ())), preferred_element_type=F32)


def _dot_tn(a, b):
    return lax.dot_general(a, b, (((0,), (0,)), ((), ())), preferred_element_type=F32)


def _norm_proj_kernel(x_ref, g_ref, w_ref, o_ref, xn_ref, *, chunk, scales):
    xn_ref[...] = _rms(x_ref[...], g_ref[...]).astype(BF16)
    for c, scale in enumerate(scales):
        cols = slice(c * chunk, (c + 1) * chunk)
        r = _dot(xn_ref[...], w_ref[:, cols])
        if scale != 1.0:
            r = r * scale
        o_ref[:, cols] = r.astype(BF16)


def _norm_proj(x2, g, w, scales, chunk):
    t, d = x2.shape
    n = w.shape[1]
    tm = _pick_tile(t, (1024, 640, 512, 256, 128))
    return pl.pallas_call(
        functools.partial(_norm_proj_kernel, chunk=chunk, scales=scales),
        grid=(t // tm,),
        in_specs=[pl.BlockSpec((tm, d), lambda i: (i, 0)),
                  _const_spec((1, d)),
                  _const_spec((d, n))],
        out_specs=pl.BlockSpec((tm, n), lambda i: (i, 0)),
        out_shape=jax.ShapeDtypeStruct((t, n), BF16),
        scratch_shapes=[pltpu.VMEM((tm, d), BF16)],
        compiler_params=_params(("parallel",)),
    )(x2, g.reshape(1, d), w)


def _res_proj_kernel(*refs, n_in):
    h_ref, o_ref = refs[0], refs[-1]
    acc = h_ref[...]
    for a_ref, w_ref in zip(refs[1:1 + n_in], refs[1 + n_in:1 + 2 * n_in]):
        acc = acc + _dot(a_ref[...], w_ref[...])
    o_ref[...] = acc


def _res_proj(h2, acts, weights):
    t, d = h2.shape
    tm = _pick_tile(t, (1024, 640, 512, 256, 128))
    in_specs = [pl.BlockSpec((tm, d), lambda i: (i, 0))]
    in_specs += [pl.BlockSpec((tm, a.shape[1]), lambda i: (i, 0)) for a in acts]
    in_specs += [_const_spec(w.shape) for w in weights]
    return pl.pallas_call(
        functools.partial(_res_proj_kernel, n_in=len(acts)),
        grid=(t // tm,),
        in_specs=in_specs,
        out_specs=pl.BlockSpec((tm, d), lambda i: (i, 0)),
        out_shape=jax.ShapeDtypeStruct((t, d), F32),
        compiler_params=_params(("parallel",)),
    )(h2, *acts, *weights)


def _ffn_kernel(*refs, tm, f, fc, final):
    if final:
        x_ref, xh_ref, g_ref, wup_ref, wc_ref, bc_ref, wd_ref, fg_ref, o_ref, xn_ref, acc_ref = refs
    else:
        x_ref, xh_ref, g_ref, wup_ref, wc_ref, bc_ref, wd_ref, o_ref, xn_ref, acc_ref = refs
    i = pl.program_id(1)
    g = g_ref[...]
    xn_ref[0:HALO, :] = _rms(xh_ref[0], g).astype(BF16)
    xn_ref[HALO:, :] = _rms(x_ref[0], g).astype(BF16)
    pos = i * tm - HALO + lax.broadcasted_iota(jnp.int32, (tm + HALO, 1), 0)
    valid = pos >= N_PAD
    for c in range(f // fc):
        cols = slice(c * fc, (c + 1) * fc)
        gate = jnp.where(valid, _dot(xn_ref[...], wup_ref[:, cols]), 0.0)
        val = _dot(xn_ref[HALO:, :], wup_ref[:, f + c * fc:f + (c + 1) * fc])
        conv = bc_ref[:, cols]
        for tap in range(CONV_WIDTH):
            lo = HALO - (CONV_WIDTH - 1) + tap
            conv = conv + gate[lo:lo + tm, :] * wc_ref[tap:tap + 1, cols]
        mid = (conv / (1.0 + jnp.exp(-conv)) * val).astype(BF16)
        contrib = _dot(mid, wd_ref[cols, :])
        if c == 0:
            acc_ref[...] = contrib
        else:
            acc_ref[...] += contrib
    y = x_ref[0] + acc_ref[...]
    if final:
        y = _rms(y, fg_ref[...])
    o_ref[0] = y


def _ffn(h, g, w_up, w_conv, b_conv, w_down, final_g=None):
    b, p, d = h.shape
    f = w_down.shape[0]
    tm = _pick_tile(p, (832, 640, 512, 256, 128))
    fc = 256
    final = final_g is not None
    in_specs = [pl.BlockSpec((1, tm, d), lambda bi, i: (bi, i, 0)),
                pl.BlockSpec((1, HALO, d),
                             lambda bi, i: (bi, jnp.maximum(i * (tm // HALO) - 1, 0), 0)),
                _const_spec((1, d)),
                _const_spec((d, 2 * f)),
                _const_spec((CONV_WIDTH, f)),
                _const_spec((1, f)),
                _const_spec((f, d))]
    args = [h, h, g.reshape(1, d), w_up, w_conv, b_conv.reshape(1, f), w_down]
    if final:
        in_specs.append(_const_spec((1, d)))
        args.append(final_g.reshape(1, d))
    return pl.pallas_call(
        functools.partial(_ffn_kernel, tm=tm, f=f, fc=fc, final=final),
        grid=(b, p // tm),
        in_specs=in_specs,
        out_specs=pl.BlockSpec((1, tm, d), lambda bi, i: (bi, i, 0)),
        out_shape=jax.ShapeDtypeStruct((b, p, d), F32),
        scratch_shapes=[pltpu.VMEM((tm + HALO, d), BF16), pltpu.VMEM((tm, d), F32)],
        compiler_params=_params(("parallel", "parallel")),
    )(*args)


def _retention_constants():
    c = BLOCK
    hh = np.arange(N_RET_HEADS, dtype=np.float64)
    log_g = np.log1p(-(2.0 ** (-5.0 - hh)))
    j = np.arange(c, dtype=np.float64)
    rel = j[:, None] - j[None, :]
    scale = HEAD_W ** -0.5
    decay = np.where(rel >= 0, np.exp(log_g[:, None, None] * np.maximum(rel, 0.0)), 0.0) * scale
    q_decay = np.broadcast_to(np.exp(log_g[:, None] * (j + 1.0))[:, :, None], (N_RET_HEADS, c, HEAD_W))
    k_decay = np.broadcast_to((np.exp(log_g[:, None] * (c - 1.0 - j)) * scale)[:, :, None],
                              (N_RET_HEADS, c, HEAD_W))
    chunk_decay = tuple(float(v) for v in np.exp(log_g * c))
    return (jnp.asarray(decay, F32), jnp.asarray(q_decay, F32), jnp.asarray(k_decay, F32), chunk_decay)


def _retention_kernel(q_ref, k_ref, v_ref, gate_ref, dmat_ref, qd_ref, kd_ref, gn_ref, o_ref, state_ref,
                      *, tr, chunk_decay):
    i = pl.program_id(1)

    @pl.when(i == 0)
    def _():
        state_ref[...] = jnp.zeros_like(state_ref)

    for c in range(tr // BLOCK):
        rows = slice(c * BLOCK, (c + 1) * BLOCK)
        if c == 0:
            pos = i * tr + lax.broadcasted_iota(jnp.int32, (BLOCK, HEAD_W), 0)
            valid = pos >= N_PAD
        for h in range(N_RET_HEADS):
            cols = slice(h * HEAD_W, (h + 1) * HEAD_W)
            q = q_ref[0, rows, cols]
            k = k_ref[0, rows, cols]
            v = v_ref[0, rows, cols]
            if c == 0:
                k = jnp.where(valid, k, jnp.zeros_like(k))
                v = jnp.where(valid, v, jnp.zeros_like(v))
            scores = _dot_nt(q, k) * dmat_ref[h]
            inner = _dot(scores.astype(BF16), v)
            state = state_ref[h]
            cross = _dot(q, state.astype(BF16)) * qd_ref[h]
            k_scaled = (k.astype(F32) * kd_ref[h]).astype(BF16)
            state_ref[h] = chunk_decay[h] * state + _dot_tn(k_scaled, v)
            o = inner + cross
            oc = o - jnp.mean(o, axis=-1, keepdims=True)
            y = oc * lax.rsqrt(jnp.mean(oc * oc, axis=-1, keepdims=True) + EPS) * gn_ref[:, cols]
            gate = gate_ref[0, rows, cols].astype(F32)
            o_ref[0, rows, cols] = (y * (gate / (1.0 + jnp.exp(-gate)))).astype(BF16)


def _retention(proj, gn):
    b, p, _ = proj.shape
    w = N_RET_HEADS * HEAD_W
    tr = _pick_tile(p, (640, 512, 256, 128))
    dmat, qd, kd, chunk_decay = _retention_constants()

    def col_spec(cb):
        return pl.BlockSpec((1, tr, w), lambda bi, i: (bi, i, cb))

    cshape = (N_RET_HEADS, BLOCK, HEAD_W)
    return pl.pallas_call(
        functools.partial(_retention_kernel, tr=tr, chunk_decay=chunk_decay),
        grid=(b, p // tr),
        in_specs=[col_spec(0), col_spec(1), col_spec(2), col_spec(3),
                  _const_spec(cshape), _const_spec(cshape), _const_spec(cshape), _const_spec((1, w))],
        out_specs=pl.BlockSpec((1, tr, w), lambda bi, i: (bi, i, 0)),
        out_shape=jax.ShapeDtypeStruct((b, p, w), BF16),
        scratch_shapes=[pltpu.VMEM(cshape, F32)],
        compiler_params=_params(("parallel", "arbitrary")),
    )(proj, proj, proj, proj, dmat, qd, kd, gn.reshape(1, w))


def _diff_attn_kernel(slope_ref, lq1_ref, lk1_ref, lq2_ref, lk2_ref, q_ref, k_ref, v_ref, gn_ref, o_ref,
                      qs_ref, m_ref, l_ref, acc_ref, *, tq, tk, lambda_init):
    h = pl.program_id(1)
    qi = pl.program_id(2)
    slope = slope_ref[h]
    q0 = qi * tq

    q = q_ref[0].astype(F32) * (DIFF_QK_DIM ** -0.5)
    lane = lax.broadcasted_iota(jnp.int32, (tq, HEAD_W), 1)
    qs_ref[0:tq, :] = jnp.where(lane < DIFF_QK_DIM, q, 0.0).astype(BF16)
    qs_ref[tq:, :] = jnp.where(lane >= DIFF_QK_DIM, q, 0.0).astype(BF16)
    m_ref[...] = jnp.full_like(m_ref, MASK_VALUE)
    l_ref[...] = jnp.zeros_like(l_ref)
    acc_ref[...] = jnp.zeros_like(acc_ref)

    def step(j, masked):
        start = pl.multiple_of(j * tk, tk)
        k = k_ref[0, pl.ds(start, tk), :]
        v = v_ref[0, pl.ds(start, tk), :]
        kpos = j * tk + lax.broadcasted_iota(jnp.int32, (1, tk), 1)
        s = _dot_nt(qs_ref[...], k) + slope * (kpos - q0).astype(F32)
        if masked:
            row = lax.broadcasted_iota(jnp.int32, (2 * tq, 1), 0)
            qpos = q0 + jnp.where(row >= tq, row - tq, row)
            s = jnp.where((qpos >= kpos) & (kpos >= N_PAD), s, MASK_VALUE)
        m_prev = m_ref[...]
        m_new = jnp.maximum(m_prev, jnp.max(s, axis=1, keepdims=True))
        p = jnp.exp(s - m_new)
        alpha = jnp.exp(m_prev - m_new)
        l_ref[...] = alpha * l_ref[...] + jnp.sum(p, axis=1, keepdims=True)
        acc_ref[...] = alpha * acc_ref[...] + _dot(p.astype(BF16), v)
        m_ref[...] = m_new

    step(0, True)

    def body(j, carry):
        step(j, False)
        return carry

    lax.fori_loop(1, qi, body, 0)

    @pl.when(qi > 0)
    def _():
        step(qi, True)

    o = acc_ref[...] / l_ref[...]
    lam = (jnp.exp(jnp.sum(lq1_ref[...] * lk1_ref[...], axis=1, keepdims=True))
           - jnp.exp(jnp.sum(lq2_ref[...] * lk2_ref[...], axis=1, keepdims=True)) + lambda_init)
    o = o[0:tq, :] - lam * o[tq:, :]
    o_ref[0] = (_rms(o, gn_ref[...]) * (1.0 - lambda_init)).astype(BF16)


def _diff_attention(proj, lam_q1, lam_k1, lam_q2, lam_k2, gn, lambda_init, col0):
    b, p, _ = proj.shape
    nh = N_DIFF_HEADS
    tq = tk = _pick_tile(p, (640, 512, 256, 128))
    slopes = jnp.asarray(2.0 ** (-8.0 * (np.arange(nh) + 1.0) / nh), F32)
    lam_spec = _const_spec((1, DIFF_QK_DIM))
    kv_block = (1, p, HEAD_W)
    return pl.pallas_call(
        functools.partial(_diff_attn_kernel, tq=tq, tk=tk, lambda_init=lambda_init),
        grid=(b, nh, p // tq),
        in_specs=[pl.BlockSpec(memory_space=pltpu.SMEM),
                  lam_spec, lam_spec, lam_spec, lam_spec,
                  pl.BlockSpec((1, tq, HEAD_W), lambda bi, h, i: (bi, i, col0 + h)),
                  pl.BlockSpec(kv_block, lambda bi, h, i: (bi, 0, col0 + nh + h)),
                  pl.BlockSpec(kv_block, lambda bi, h, i: (bi, 0, col0 + 2 * nh + h)),
                  pl.BlockSpec((1, HEAD_W), lambda bi, h, i: (0, h))],
        out_specs=pl.BlockSpec((1, tq, HEAD_W), lambda bi, h, i: (bi, i, h)),
        out_shape=jax.ShapeDtypeStruct((b, p, nh * HEAD_W), BF16),
        scratch_shapes=[pltpu.VMEM((2 * tq, HEAD_W), BF16),
                        pltpu.VMEM((2 * tq, 1), F32),
                        pltpu.VMEM((2 * tq, 1), F32),
                        pltpu.VMEM((2 * tq, HEAD_W), F32)],
        compiler_params=_params(("parallel", "parallel", "arbitrary")),
    )(slopes, lam_q1.reshape(1, -1), lam_k1.reshape(1, -1), lam_q2.reshape(1, -1), lam_k2.reshape(1, -1),
      proj, proj, proj, gn.reshape(1, -1))


def _cumsum_blocks(tk):
    blocks, start = [], 0
    while start < tk:
        width = 256 if tk - start >= 256 else tk - start
        blocks.append((start, width))
        start += width
    return tuple(blocks)


def _sb_attn_kernel(q_ref, k_ref, v_ref, u_ref, o_ref, carry_ref, acc_ref, *, tq, tk, blocks):
    qi = pl.program_id(2)
    q0 = qi * tq
    carry_ref[...] = jnp.zeros_like(carry_ref)
    acc_ref[...] = jnp.zeros_like(acc_ref)

    def step(j, masked):
        start = pl.multiple_of(j * tk, tk)
        k = k_ref[0, pl.ds(start, tk), :]
        v = v_ref[0, pl.ds(start, tk), :]
        z = _dot_nt(q_ref[0], k)
        softplus = jnp.maximum(z, 0.0) + jnp.log1p(jnp.exp(-jnp.abs(z)))
        log_1m = -softplus
        if masked:
            qpos = q0 + lax.broadcasted_iota(jnp.int32, (tq, 1), 0)
            kpos = j * tk + lax.broadcasted_iota(jnp.int32, (1, tk), 1)
            mask = (qpos > kpos) & (kpos >= N_PAD)
            log_1m = jnp.where(mask, log_1m, 0.0)
        run = carry_ref[...]
        later = [None] * len(blocks)
        for idx in reversed(range(len(blocks))):
            b0, width = blocks[idx]
            part = log_1m[:, b0:b0 + width]
            hi = part.astype(BF16)
            lo = (part - hi.astype(F32)).astype(BF16)
            u = u_ref[0:width, 0:width]
            later[idx] = _dot(hi, u) + _dot(lo, u) + run
            run = run + jnp.sum(part, axis=1, keepdims=True)
        carry_ref[...] = run
        w = jnp.exp(z - softplus + jnp.concatenate(later, axis=1))
        if masked:
            w = jnp.where(mask, w, 0.0)
        acc_ref[...] += _dot(w.astype(BF16), v)

    @pl.when(qi > 0)
    def _():
        step(qi, True)

    def body(t, carry):
        step(qi - t, False)
        return carry

    lax.fori_loop(1, qi, body, 0)
    step(0, True)
    o_ref[0] = acc_ref[...].astype(BF16)


def _sb_attention(qkv):
    b, p, _ = qkv.shape
    nh = N_SB_HEADS
    tq = tk = _pick_tile(p, (640, 512, 256, 128))
    blocks = _cumsum_blocks(tk)
    umax = max(wd for _, wd in blocks)
    jj = np.arange(umax)
    u = jnp.asarray(jj[:, None] > jj[None, :], BF16)
    kv_block = (1, p, HEAD_W)
    return pl.pallas_call(
        functools.partial(_sb_attn_kernel, tq=tq, tk=tk, blocks=blocks),
        grid=(b, nh, p // tq),
        in_specs=[pl.BlockSpec((1, tq, HEAD_W), lambda bi, h, i: (bi, i, h)),
                  pl.BlockSpec(kv_block, lambda bi, h, i: (bi, 0, nh + h)),
                  pl.BlockSpec(kv_block, lambda bi, h, i: (bi, 0, 2 * nh + h)),
                  _const_spec((umax, umax))],
        out_specs=pl.BlockSpec((1, tq, HEAD_W), lambda bi, h, i: (bi, i, h)),
        out_shape=jax.ShapeDtypeStruct((b, p, nh * HEAD_W), BF16),
        scratch_shapes=[pltpu.VMEM((tq, 1), F32), pltpu.VMEM((tq, HEAD_W), F32)],
        compiler_params=_params(("parallel", "parallel", "arbitrary")),
    )(qkv, qkv, qkv, u)


def kernel(x, meta_tokens, mix_norm, ffn_norm, ffn_up, ffn_conv, ffn_conv_b, ffn_down, ab_w_in, ab_ret_norm,
           ab_diff_norm, ab_lam_q1, ab_lam_k1, ab_lam_q2, ab_lam_k2, ab_w_out, c_w_in, c_w_out, final_norm):
    b, _, d = x.shape
    depth = mix_norm.shape[0]
    pad = jnp.zeros((b, N_PAD, d), x.dtype)
    meta = jnp.broadcast_to(meta_tokens[None].astype(x.dtype), (b, N_META, d))
    h = jnp.concatenate([pad, meta, x], axis=1)
    p = h.shape[1]
    t = b * p
    ret_w = N_RET_HEADS * HEAD_W
    for i in range(depth):
        if i % 2 == 0:
            e = i // 2
            lambda_init = 0.8 - 0.6 * math.exp(-0.3 * i)
            w_in = ab_w_in[e].astype(BF16)
            proj = _norm_proj(h.reshape(t, d), mix_norm[i], w_in, (1.0,) * (w_in.shape[1] // 512), 512)
            proj = proj.reshape(b, p, -1)
            ret = _retention(proj, ab_ret_norm[e])
            dif = _diff_attention(proj, ab_lam_q1[e], ab_lam_k1[e], ab_lam_q2[e], ab_lam_k2[e],
                                  ab_diff_norm[e], lambda_init, 4 * N_RET_HEADS)
            w_out = ab_w_out[e].astype(BF16)
            h = _res_proj(h.reshape(t, d), [ret.reshape(t, -1), dif.reshape(t, -1)],
                          [w_out[:ret_w], w_out[ret_w:]]).reshape(b, p, d)
        else:
            o = i // 2
            w_in = c_w_in[o].astype(BF16)
            n_chunks = w_in.shape[1] // 512
            q_chunks = N_SB_HEADS * HEAD_W // 512
            scales = tuple(HEAD_W ** -0.5 if c < q_chunks else 1.0 for c in range(n_chunks))
            qkv = _norm_proj(h.reshape(t, d), mix_norm[i], w_in, scales, 512).reshape(b, p, -1)
            att = _sb_attention(qkv)
            h = _res_proj(h.reshape(t, d), [att.reshape(t, -1)], [c_w_out[o].astype(BF16)]).reshape(b, p, d)
        h = _ffn(h, ffn_norm[i], ffn_up[i].astype(BF16), ffn_conv[i], ffn_conv_b[i], ffn_down[i].astype(BF16),
                 final_norm if i == depth - 1 else None)
    return h[:, BLOCK:, :]
```

```python
import functools
import math

import numpy as np
import jax
import jax.numpy as jnp
from jax import lax
from jax.experimental import pallas as pl
from jax.experimental.pallas import tpu as pltpu

F32 = jnp.float32
BF16 = jnp.bfloat16

N_META = 16
BLOCK = 128
N_PAD = BLOCK - N_META
EPS = 1e-6
MASK_VALUE = -1e30
HEAD_W = 128
N_RET_HEADS = 4
N_DIFF_HEADS = 4
DIFF_QK_DIM = 64
N_SB_HEADS = 8
CONV_WIDTH = 3
HALO = 16
LOG2_E = math.log2(math.e)
INV_LN2 = 1.0 / math.log(2.0)
SB_EXIT_LOG2 = 160.0

VMEM_LIMIT_BYTES = 56 * 1024 * 1024


def _pick_tile(n, candidates):
    for c in candidates:
        if n % c == 0:
            return c
    raise ValueError(f"no tile for {n} in {candidates}")


def _params(sem):
    return pltpu.CompilerParams(dimension_semantics=sem, vmem_limit_bytes=VMEM_LIMIT_BYTES)


def _const_spec(shape):
    return pl.BlockSpec(shape, lambda *_: (0,) * len(shape), pipeline_mode=pl.Buffered(1))


def _rms(x, g):
    return x * lax.rsqrt(jnp.mean(x * x, axis=-1, keepdims=True) + EPS) * g


def _dot(a, b):
    return jnp.dot(a, b, preferred_element_type=F32)


def _dot_nt(a, b):
    return lax.dot_general(a, b, (((1,), (1,)), ((), ())), preferred_element_type=F32)


def _dot_tn(a, b):
    return lax.dot_general(a, b, (((0,), (0,)), ((), ())), preferred_element_type=F32)


def _norm_proj_kernel(x_ref, g_ref, w_ref, o_ref, xn_ref, *, chunk, scales):
    xn_ref[...] = _rms(x_ref[...], g_ref[...]).astype(BF16)
    for c, scale in enumerate(scales):
        cols = slice(c * chunk, (c + 1) * chunk)
        r = _dot(xn_ref[...], w_ref[:, cols])
        if scale != 1.0:
            r = r * scale
        o_ref[:, cols] = r.astype(BF16)


def _norm_proj(x2, g, w, scales, chunk):
    t, d = x2.shape
    n = w.shape[1]
    tm = _pick_tile(t, (1024, 640, 512, 256, 128))
    return pl.pallas_call(
        functools.partial(_norm_proj_kernel, chunk=chunk, scales=scales),
        grid=(t // tm,),
        in_specs=[pl.BlockSpec((tm, d), lambda i: (i, 0)),
                  _const_spec((1, d)),
                  _const_spec((d, n))],
        out_specs=pl.BlockSpec((tm, n), lambda i: (i, 0)),
        out_shape=jax.ShapeDtypeStruct((t, n), BF16),
        scratch_shapes=[pltpu.VMEM((tm, d), BF16)],
        compiler_params=_params(("parallel",)),
    )(x2, g.reshape(1, d), w)


def _res_proj_kernel(*refs, n_in):
    h_ref, o_ref = refs[0], refs[-1]
    acc = h_ref[...]
    for a_ref, w_ref in zip(refs[1:1 + n_in], refs[1 + n_in:1 + 2 * n_in]):
        acc = acc + _dot(a_ref[...], w_ref[...])
    o_ref[...] = acc


def _res_proj(h2, acts, weights):
    t, d = h2.shape
    tm = _pick_tile(t, (1024, 640, 512, 256, 128))
    in_specs = [pl.BlockSpec((tm, d), lambda i: (i, 0))]
    in_specs += [pl.BlockSpec((tm, a.shape[1]), lambda i: (i, 0)) for a in acts]
    in_specs += [_const_spec(w.shape) for w in weights]
    return pl.pallas_call(
        functools.partial(_res_proj_kernel, n_in=len(acts)),
        grid=(t // tm,),
        in_specs=in_specs,
        out_specs=pl.BlockSpec((tm, d), lambda i: (i, 0)),
        out_shape=jax.ShapeDtypeStruct((t, d), F32),
        compiler_params=_params(("parallel",)),
    )(h2, *acts, *weights)


def _ffn_kernel(*refs, tm, f, fc, final):
    if final:
        x_ref, xh_ref, g_ref, wup_ref, wc_ref, bc_ref, wd_ref, fg_ref, o_ref, xn_ref, acc_ref = refs
    else:
        x_ref, xh_ref, g_ref, wup_ref, wc_ref, bc_ref, wd_ref, o_ref, xn_ref, acc_ref = refs
    i = pl.program_id(1)
    g = g_ref[...]
    xn_ref[0:HALO, :] = _rms(xh_ref[0], g).astype(BF16)
    xn_ref[HALO:, :] = _rms(x_ref[0], g).astype(BF16)
    pos = i * tm - HALO + lax.broadcasted_iota(jnp.int32, (tm + HALO, 1), 0)
    valid = pos >= N_PAD
    for c in range(f // fc):
        cols = slice(c * fc, (c + 1) * fc)
        gate = jnp.where(valid, _dot(xn_ref[...], wup_ref[:, cols]), 0.0)
        val = _dot(xn_ref[HALO:, :], wup_ref[:, f + c * fc:f + (c + 1) * fc])
        conv = bc_ref[:, cols]
        for tap in range(CONV_WIDTH):
            lo = HALO - (CONV_WIDTH - 1) + tap
            conv = conv + gate[lo:lo + tm, :] * wc_ref[tap:tap + 1, cols]
        mid = (conv / (1.0 + jnp.exp(-conv)) * val).astype(BF16)
        contrib = _dot(mid, wd_ref[cols, :])
        if c == 0:
            acc_ref[...] = contrib
        else:
            acc_ref[...] += contrib
    y = x_ref[0] + acc_ref[...]
    if final:
        y = _rms(y, fg_ref[...])
    o_ref[0] = y


def _ffn(h, g, w_up, w_conv, b_conv, w_down, final_g=None):
    b, p, d = h.shape
    f = w_down.shape[0]
    tm = _pick_tile(p, (832, 640, 512, 256, 128))
    fc = 256
    final = final_g is not None
    in_specs = [pl.BlockSpec((1, tm, d), lambda bi, i: (bi, i, 0)),
                pl.BlockSpec((1, HALO, d),
                             lambda bi, i: (bi, jnp.maximum(i * (tm // HALO) - 1, 0), 0)),
                _const_spec((1, d)),
                _const_spec((d, 2 * f)),
                _const_spec((CONV_WIDTH, f)),
                _const_spec((1, f)),
                _const_spec((f, d))]
    args = [h, h, g.reshape(1, d), w_up, w_conv, b_conv.reshape(1, f), w_down]
    if final:
        in_specs.append(_const_spec((1, d)))
        args.append(final_g.reshape(1, d))
    return pl.pallas_call(
        functools.partial(_ffn_kernel, tm=tm, f=f, fc=fc, final=final),
        grid=(b, p // tm),
        in_specs=in_specs,
        out_specs=pl.BlockSpec((1, tm, d), lambda bi, i: (bi, i, 0)),
        out_shape=jax.ShapeDtypeStruct((b, p, d), F32),
        scratch_shapes=[pltpu.VMEM((tm + HALO, d), BF16), pltpu.VMEM((tm, d), F32)],
        compiler_params=_params(("parallel", "parallel")),
    )(*args)


def _retention_constants():
    c = BLOCK
    hh = np.arange(N_RET_HEADS, dtype=np.float64)
    log_g = np.log1p(-(2.0 ** (-5.0 - hh)))
    j = np.arange(c, dtype=np.float64)
    rel = j[:, None] - j[None, :]
    scale = HEAD_W ** -0.5
    decay = np.where(rel >= 0, np.exp(log_g[:, None, None] * np.maximum(rel, 0.0)), 0.0) * scale
    q_decay = np.broadcast_to(np.exp(log_g[:, None] * (j + 1.0))[:, :, None], (N_RET_HEADS, c, HEAD_W))
    k_decay = np.broadcast_to((np.exp(log_g[:, None] * (c - 1.0 - j)) * scale)[:, :, None],
                              (N_RET_HEADS, c, HEAD_W))
    chunk_decay = tuple(float(v) for v in np.exp(log_g * c))
    return (jnp.asarray(decay, F32), jnp.asarray(q_decay, F32), jnp.asarray(k_decay, F32), chunk_decay)


def _retention_kernel(q_ref, k_ref, v_ref, gate_ref, dmat_ref, qd_ref, kd_ref, gn_ref, o_ref, state_ref,
                      *, tr, chunk_decay):
    i = pl.program_id(1)

    @pl.when(i == 0)
    def _():
        state_ref[...] = jnp.zeros_like(state_ref)

    for c in range(tr // BLOCK):
        rows = slice(c * BLOCK, (c + 1) * BLOCK)
        if c == 0:
            pos = i * tr + lax.broadcasted_iota(jnp.int32, (BLOCK, HEAD_W), 0)
            valid = pos >= N_PAD
        for h in range(N_RET_HEADS):
            cols = slice(h * HEAD_W, (h + 1) * HEAD_W)
            q = q_ref[0, rows, cols]
            k = k_ref[0, rows, cols]
            v = v_ref[0, rows, cols]
            if c == 0:
                k = jnp.where(valid, k, jnp.zeros_like(k))
                v = jnp.where(valid, v, jnp.zeros_like(v))
            scores = _dot_nt(q, k) * dmat_ref[h]
            inner = _dot(scores.astype(BF16), v)
            state = state_ref[h]
            cross = _dot(q, state.astype(BF16)) * qd_ref[h]
            k_scaled = (k.astype(F32) * kd_ref[h]).astype(BF16)
            state_ref[h] = chunk_decay[h] * state + _dot_tn(k_scaled, v)
            o = inner + cross
            oc = o - jnp.mean(o, axis=-1, keepdims=True)
            y = oc * lax.rsqrt(jnp.mean(oc * oc, axis=-1, keepdims=True) + EPS) * gn_ref[:, cols]
            gate = gate_ref[0, rows, cols].astype(F32)
            o_ref[0, rows, cols] = (y * (gate / (1.0 + jnp.exp(-gate)))).astype(BF16)


def _retention(proj, gn):
    b, p, _ = proj.shape
    w = N_RET_HEADS * HEAD_W
    tr = _pick_tile(p, (640, 512, 256, 128))
    dmat, qd, kd, chunk_decay = _retention_constants()

    def col_spec(cb):
        return pl.BlockSpec((1, tr, w), lambda bi, i: (bi, i, cb))

    cshape = (N_RET_HEADS, BLOCK, HEAD_W)
    return pl.pallas_call(
        functools.partial(_retention_kernel, tr=tr, chunk_decay=chunk_decay),
        grid=(b, p // tr),
        in_specs=[col_spec(0), col_spec(1), col_spec(2), col_spec(3),
                  _const_spec(cshape), _const_spec(cshape), _const_spec(cshape), _const_spec((1, w))],
        out_specs=pl.BlockSpec((1, tr, w), lambda bi, i: (bi, i, 0)),
        out_shape=jax.ShapeDtypeStruct((b, p, w), BF16),
        scratch_shapes=[pltpu.VMEM(cshape, F32)],
        compiler_params=_params(("parallel", "arbitrary")),
    )(proj, proj, proj, proj, dmat, qd, kd, gn.reshape(1, w))


def _diff_attn_kernel(slope_ref, lq1_ref, lk1_ref, lq2_ref, lk2_ref, q_ref, k_ref, v_ref, gn_ref, o_ref,
                      qs_ref, m_ref, l_ref, acc_ref, *, tq, tk, rsub, lambda_init):
    h = pl.program_id(1)
    qi = pl.program_id(2)
    slope = slope_ref[h] * LOG2_E
    q0 = qi * tq

    q = q_ref[0]
    lane = lax.broadcasted_iota(jnp.int32, (tq, HEAD_W), 1)
    qs_ref[0:tq, :] = jnp.where(lane < DIFF_QK_DIM, q, jnp.zeros_like(q))
    qs_ref[tq:, :] = jnp.where(lane >= DIFF_QK_DIM, q, jnp.zeros_like(q))
    m_ref[...] = jnp.full_like(m_ref, MASK_VALUE)
    l_ref[...] = jnp.zeros_like(l_ref)
    acc_ref[...] = jnp.zeros_like(acc_ref)

    def step(j, masked):
        start = pl.multiple_of(j * tk, tk)
        k = k_ref[0, pl.ds(start, tk), :]
        v = v_ref[0, pl.ds(start, tk), :]
        kpos = j * tk + lax.broadcasted_iota(jnp.int32, (1, tk), 1)
        bias = slope * (kpos - q0).astype(F32)
        def scores(r):
            s = _dot_nt(qs_ref[r * rsub:(r + 1) * rsub, :], k) + bias
            if masked:
                qpos = q0 + (r * rsub) % tq + lax.broadcasted_iota(jnp.int32, (rsub, 1), 0)
                s = jnp.where((qpos >= kpos) & (kpos >= N_PAD), s, MASK_VALUE)
            return s

        n_sub = 2 * tq // rsub
        s_next = scores(0)
        for r in range(n_sub):
            rows = slice(r * rsub, (r + 1) * rsub)
            s = s_next
            if r + 1 < n_sub:
                s_next = scores(r + 1)
            m_prev = m_ref[rows, :]
            m_new = jnp.maximum(m_prev, jnp.max(s, axis=1, keepdims=True))
            p = jnp.exp2(s - m_new)
            alpha = jnp.exp2(m_prev - m_new)
            l_ref[rows, :] = alpha * l_ref[rows, :] + jnp.sum(p, axis=1, keepdims=True)
            acc_ref[rows, :] = alpha * acc_ref[rows, :] + _dot(p.astype(BF16), v)
            m_ref[rows, :] = m_new

    step(0, True)

    def body(j, carry):
        step(j, False)
        return carry

    lax.fori_loop(1, qi, body, 0)

    @pl.when(qi > 0)
    def _():
        step(qi, True)

    o = acc_ref[...] / l_ref[...]
    lam = (jnp.exp(jnp.sum(lq1_ref[...] * lk1_ref[...], axis=1, keepdims=True))
           - jnp.exp(jnp.sum(lq2_ref[...] * lk2_ref[...], axis=1, keepdims=True)) + lambda_init)
    o = o[0:tq, :] - lam * o[tq:, :]
    o_ref[0] = (_rms(o, gn_ref[...]) * (1.0 - lambda_init)).astype(BF16)


def _diff_attention(proj, lam_q1, lam_k1, lam_q2, lam_k2, gn, lambda_init, col0):
    b, p, _ = proj.shape
    nh = N_DIFF_HEADS
    tq = tk = _pick_tile(p, (640, 512, 256, 128))
    rsub = 320 if tq % 320 == 0 else tq
    slopes = jnp.asarray(2.0 ** (-8.0 * (np.arange(nh) + 1.0) / nh), F32)
    lam_spec = _const_spec((1, DIFF_QK_DIM))
    kv_block = (1, p, HEAD_W)
    return pl.pallas_call(
        functools.partial(_diff_attn_kernel, tq=tq, tk=tk, rsub=rsub, lambda_init=lambda_init),
        grid=(b, nh, p // tq),
        in_specs=[pl.BlockSpec(memory_space=pltpu.SMEM),
                  lam_spec, lam_spec, lam_spec, lam_spec,
                  pl.BlockSpec((1, tq, HEAD_W), lambda bi, h, i: (bi, i, col0 + h)),
                  pl.BlockSpec(kv_block, lambda bi, h, i: (bi, 0, col0 + nh + h)),
                  pl.BlockSpec(kv_block, lambda bi, h, i: (bi, 0, col0 + 2 * nh + h)),
                  pl.BlockSpec((1, HEAD_W), lambda bi, h, i: (0, h))],
        out_specs=pl.BlockSpec((1, tq, HEAD_W), lambda bi, h, i: (bi, i, h)),
        out_shape=jax.ShapeDtypeStruct((b, p, nh * HEAD_W), BF16),
        scratch_shapes=[pltpu.VMEM((2 * tq, HEAD_W), BF16),
                        pltpu.VMEM((2 * tq, 1), F32),
                        pltpu.VMEM((2 * tq, 1), F32),
                        pltpu.VMEM((2 * tq, HEAD_W), F32)],
        compiler_params=_params(("parallel", "parallel", "arbitrary")),
    )(slopes, lam_q1.reshape(1, -1), lam_k1.reshape(1, -1), lam_q2.reshape(1, -1), lam_k2.reshape(1, -1),
      proj, proj, proj, gn.reshape(1, -1))


def _cumsum_blocks(tk):
    blocks, start = [], 0
    while start < tk:
        width = 256 if tk - start >= 256 else tk - start
        blocks.append((start, width))
        start += width
    return tuple(blocks)


def _sb_attn_kernel(q_ref, k_ref, v_ref, u_ref, o_ref, run_ref, acc_ref, *, tq, tk, blocks, rsub):
    qi = pl.program_id(2)
    q0 = qi * tq
    run_ref[...] = jnp.zeros_like(run_ref)
    acc_ref[...] = jnp.zeros_like(acc_ref)

    def step(j, masked):
        start = pl.multiple_of(j * tk, tk)
        k = k_ref[0, pl.ds(start, tk), :]
        v = v_ref[0, pl.ds(start, tk), :]
        if masked:
            kpos = j * tk + lax.broadcasted_iota(jnp.int32, (1, tk), 1)
        for r in range(tq // rsub):
            rows = slice(r * rsub, (r + 1) * rsub)
            z = _dot_nt(q_ref[0, rows, :], k)
            neg_abs = lax.bitcast_convert_type(
                lax.bitcast_convert_type(z, jnp.uint32) | jnp.uint32(0x80000000), F32)
            lg = jnp.log(1.0 + jnp.exp2(neg_abs)) * INV_LN2
            softplus = jnp.maximum(z, 0.0) + lg
            log_beta = jnp.minimum(z, 0.0) - lg
            if masked:
                qpos = q0 + r * rsub + lax.broadcasted_iota(jnp.int32, (rsub, 1), 0)
                mask = (qpos > kpos) & (kpos >= N_PAD)
                softplus = jnp.where(mask, softplus, 0.0)
            run = run_ref[rows, :]
            later = [None] * len(blocks)
            for idx in reversed(range(len(blocks))):
                b0, width = blocks[idx]
                part = softplus[:, b0:b0 + width]
                hi = part.astype(BF16)
                lo = (part - hi.astype(F32)).astype(BF16)
                u = u_ref[0:width, 0:width]
                later[idx] = _dot(hi, u) + _dot(lo, u) + run
                run = run - jnp.sum(part, axis=1, keepdims=True)
            run_ref[rows, :] = run
            w = jnp.exp2(log_beta + jnp.concatenate(later, axis=1))
            if masked:
                w = jnp.where(mask, w, 0.0)
            acc_ref[rows, :] += _dot(w.astype(BF16), v)

    def more_needed():
        return jnp.max(run_ref[...]) > -SB_EXIT_LOG2

    step(qi, True)

    def cond(carry):
        j, go = carry
        return (j >= 1) & go

    def body(carry):
        j, _ = carry
        step(j, False)
        return j - 1, more_needed()

    j_end, go = lax.while_loop(cond, body, (qi - 1, more_needed()))

    @pl.when((j_end == 0) & go)
    def _():
        step(0, True)

    o_ref[0] = acc_ref[...].astype(BF16)


def _sb_attention(qkv):
    b, p, _ = qkv.shape
    nh = N_SB_HEADS
    tq = tk = _pick_tile(p, (640, 512, 256, 128))
    rsub = 320 if tq % 320 == 0 else tq
    blocks = _cumsum_blocks(tk)
    umax = max(wd for _, wd in blocks)
    jj = np.arange(umax)
    u = jnp.asarray(-(jj[:, None] > jj[None, :]).astype(np.float32), BF16)
    kv_block = (1, p, HEAD_W)
    return pl.pallas_call(
        functools.partial(_sb_attn_kernel, tq=tq, tk=tk, blocks=blocks, rsub=rsub),
        grid=(b, nh, p // tq),
        in_specs=[pl.BlockSpec((1, tq, HEAD_W), lambda bi, h, i: (bi, i, h)),
                  pl.BlockSpec(kv_block, lambda bi, h, i: (bi, 0, nh + h)),
                  pl.BlockSpec(kv_block, lambda bi, h, i: (bi, 0, 2 * nh + h)),
                  _const_spec((umax, umax))],
        out_specs=pl.BlockSpec((1, tq, HEAD_W), lambda bi, h, i: (bi, i, h)),
        out_shape=jax.ShapeDtypeStruct((b, p, nh * HEAD_W), BF16),
        scratch_shapes=[pltpu.VMEM((tq, 1), F32), pltpu.VMEM((tq, HEAD_W), F32)],
        compiler_params=_params(("parallel", "parallel", "arbitrary")),
    )(qkv, qkv, qkv, u)


def kernel(x, meta_tokens, mix_norm, ffn_norm, ffn_up, ffn_conv, ffn_conv_b, ffn_down, ab_w_in, ab_ret_norm,
           ab_diff_norm, ab_lam_q1, ab_lam_k1, ab_lam_q2, ab_lam_k2, ab_w_out, c_w_in, c_w_out, final_norm):
    b, _, d = x.shape
    depth = mix_norm.shape[0]
    pad = jnp.zeros((b, N_PAD, d), x.dtype)
    meta = jnp.broadcast_to(meta_tokens[None].astype(x.dtype), (b, N_META, d))
    h = jnp.concatenate([pad, meta, x], axis=1)
    p = h.shape[1]
    t = b * p
    ret_w = N_RET_HEADS * HEAD_W
    for i in range(depth):
        if i % 2 == 0:
            e = i // 2
            lambda_init = 0.8 - 0.6 * math.exp(-0.3 * i)
            w_in = ab_w_in[e].astype(BF16)
            scales = (1.0, 1.0, 1.0, 1.0, DIFF_QK_DIM ** -0.5 * LOG2_E, 1.0, 1.0)
            proj = _norm_proj(h.reshape(t, d), mix_norm[i], w_in, scales, 512)
            proj = proj.reshape(b, p, -1)
            ret = _retention(proj, ab_ret_norm[e])
            dif = _diff_attention(proj, ab_lam_q1[e], ab_lam_k1[e], ab_lam_q2[e], ab_lam_k2[e],
                                  ab_diff_norm[e], lambda_init, 4 * N_RET_HEADS)
            w_out = ab_w_out[e].astype(BF16)
            h = _res_proj(h.reshape(t, d), [ret.reshape(t, -1), dif.reshape(t, -1)],
                          [w_out[:ret_w], w_out[ret_w:]]).reshape(b, p, d)
        else:
            o = i // 2
            w_in = c_w_in[o].astype(BF16)
            n_chunks = w_in.shape[1] // 512
            q_chunks = N_SB_HEADS * HEAD_W // 512
            scales = tuple(HEAD_W ** -0.5 * LOG2_E if c < q_chunks else 1.0 for c in range(n_chunks))
            qkv = _norm_proj(h.reshape(t, d), mix_norm[i], w_in, scales, 512).reshape(b, p, -1)
            att = _sb_attention(qkv)
            h = _res_proj(h.reshape(t, d), [att.reshape(t, -1)], [c_w_out[o].astype(BF16)]).reshape(b, p, d)
        h = _ffn(h, ffn_norm[i], ffn_up[i].astype(BF16), ffn_conv[i], ffn_conv_b[i], ffn_down[i].astype(BF16),
                 final_norm if i == depth - 1 else None)
    return h[:, BLOCK:, :]
```

```python
import functools
import math

import numpy as np
import jax
import jax.numpy as jnp
from jax import lax
from jax.experimental import pallas as pl
from jax.experimental.pallas import tpu as pltpu

F32 = jnp.float32
BF16 = jnp.bfloat16

N_META = 16
BLOCK = 128
N_PAD = BLOCK - N_META
EPS = 1e-6
MASK_VALUE = -1e30
HEAD_W = 128
N_RET_HEADS = 4
N_DIFF_HEADS = 4
DIFF_QK_DIM = 64
N_SB_HEADS = 8
CONV_WIDTH = 3
HALO = 16
LOG2_E = math.log2(math.e)
INV_LN2 = 1.0 / math.log(2.0)
ZERO_PROB_LOG2 = 160.0
SB_KEY_CHUNK = 256
SB_TOP_ROWS = 256

VMEM_LIMIT_BYTES = 56 * 1024 * 1024


def _pick_tile(n, candidates):
    for c in candidates:
        if n % c == 0:
            return c
    raise ValueError(f"no tile for {n} in {candidates}")


def _params(sem):
    return pltpu.CompilerParams(dimension_semantics=sem, vmem_limit_bytes=VMEM_LIMIT_BYTES)


def _const_spec(shape):
    return pl.BlockSpec(shape, lambda *_: (0,) * len(shape), pipeline_mode=pl.Buffered(1))


def _rms(x, g):
    return x * lax.rsqrt(jnp.mean(x * x, axis=-1, keepdims=True) + EPS) * g


def _dot(a, b):
    return jnp.dot(a, b, preferred_element_type=F32)


def _dot_nt(a, b):
    return lax.dot_general(a, b, (((1,), (1,)), ((), ())), preferred_element_type=F32)


def _dot_tn(a, b):
    return lax.dot_general(a, b, (((0,), (0,)), ((), ())), preferred_element_type=F32)


def _norm_proj_kernel(x_ref, g_ref, w_ref, o_ref, xn_ref, *, chunk, scales):
    xn_ref[...] = _rms(x_ref[...], g_ref[...]).astype(BF16)
    for c, scale in enumerate(scales):
        cols = slice(c * chunk, (c + 1) * chunk)
        r = _dot(xn_ref[...], w_ref[:, cols])
        if scale != 1.0:
            r = r * scale
        o_ref[:, cols] = r.astype(BF16)


def _norm_proj(x2, g, w, scales, chunk):
    t, d = x2.shape
    n = w.shape[1]
    tm = _pick_tile(t, (1024, 640, 512, 256, 128))
    return pl.pallas_call(
        functools.partial(_norm_proj_kernel, chunk=chunk, scales=scales),
        grid=(t // tm,),
        in_specs=[pl.BlockSpec((tm, d), lambda i: (i, 0)),
                  _const_spec((1, d)),
                  _const_spec((d, n))],
        out_specs=pl.BlockSpec((tm, n), lambda i: (i, 0)),
        out_shape=jax.ShapeDtypeStruct((t, n), BF16),
        scratch_shapes=[pltpu.VMEM((tm, d), BF16)],
        compiler_params=_params(("parallel",)),
    )(x2, g.reshape(1, d), w)


def _res_proj_kernel(*refs, n_in):
    h_ref, o_ref = refs[0], refs[-1]
    acc = h_ref[...]
    for a_ref, w_ref in zip(refs[1:1 + n_in], refs[1 + n_in:1 + 2 * n_in]):
        acc = acc + _dot(a_ref[...], w_ref[...])
    o_ref[...] = acc


def _res_proj(h2, acts, weights):
    t, d = h2.shape
    tm = _pick_tile(t, (1024, 640, 512, 256, 128))
    in_specs = [pl.BlockSpec((tm, d), lambda i: (i, 0))]
    in_specs += [pl.BlockSpec((tm, a.shape[1]), lambda i: (i, 0)) for a in acts]
    in_specs += [_const_spec(w.shape) for w in weights]
    return pl.pallas_call(
        functools.partial(_res_proj_kernel, n_in=len(acts)),
        grid=(t // tm,),
        in_specs=in_specs,
        out_specs=pl.BlockSpec((tm, d), lambda i: (i, 0)),
        out_shape=jax.ShapeDtypeStruct((t, d), F32),
        compiler_params=_params(("parallel",)),
    )(h2, *acts, *weights)


def _ffn_kernel(*refs, tm, f, fc, final):
    if final:
        x_ref, xh_ref, g_ref, wup_ref, wc_ref, bc_ref, wd_ref, fg_ref, o_ref, xn_ref, acc_ref = refs
    else:
        x_ref, xh_ref, g_ref, wup_ref, wc_ref, bc_ref, wd_ref, o_ref, xn_ref, acc_ref = refs
    i = pl.program_id(1)
    g = g_ref[...]
    xn_ref[0:HALO, :] = _rms(xh_ref[0], g).astype(BF16)
    xn_ref[HALO:, :] = _rms(x_ref[0], g).astype(BF16)
    pos = i * tm - HALO + lax.broadcasted_iota(jnp.int32, (tm + HALO, 1), 0)
    valid = pos >= N_PAD
    for c in range(f // fc):
        cols = slice(c * fc, (c + 1) * fc)
        gate = jnp.where(valid, _dot(xn_ref[...], wup_ref[:, cols]), 0.0)
        val = _dot(xn_ref[HALO:, :], wup_ref[:, f + c * fc:f + (c + 1) * fc])
        conv = bc_ref[:, cols]
        for tap in range(CONV_WIDTH):
            lo = HALO - (CONV_WIDTH - 1) + tap
            conv = conv + gate[lo:lo + tm, :] * wc_ref[tap:tap + 1, cols]
        mid = (conv / (1.0 + jnp.exp(-conv)) * val).astype(BF16)
        contrib = _dot(mid, wd_ref[cols, :])
        if c == 0:
            acc_ref[...] = contrib
        else:
            acc_ref[...] += contrib
    y = x_ref[0] + acc_ref[...]
    if final:
        y = _rms(y, fg_ref[...])
    o_ref[0] = y


def _ffn(h, g, w_up, w_conv, b_conv, w_down, final_g=None):
    b, p, d = h.shape
    f = w_down.shape[0]
    tm = _pick_tile(p, (832, 640, 512, 256, 128))
    fc = 256
    final = final_g is not None
    in_specs = [pl.BlockSpec((1, tm, d), lambda bi, i: (bi, i, 0)),
                pl.BlockSpec((1, HALO, d),
                             lambda bi, i: (bi, jnp.maximum(i * (tm // HALO) - 1, 0), 0)),
                _const_spec((1, d)),
                _const_spec((d, 2 * f)),
                _const_spec((CONV_WIDTH, f)),
                _const_spec((1, f)),
                _const_spec((f, d))]
    args = [h, h, g.reshape(1, d), w_up, w_conv, b_conv.reshape(1, f), w_down]
    if final:
        in_specs.append(_const_spec((1, d)))
        args.append(final_g.reshape(1, d))
    return pl.pallas_call(
        functools.partial(_ffn_kernel, tm=tm, f=f, fc=fc, final=final),
        grid=(b, p // tm),
        in_specs=in_specs,
        out_specs=pl.BlockSpec((1, tm, d), lambda bi, i: (bi, i, 0)),
        out_shape=jax.ShapeDtypeStruct((b, p, d), F32),
        scratch_shapes=[pltpu.VMEM((tm + HALO, d), BF16), pltpu.VMEM((tm, d), F32)],
        compiler_params=_params(("parallel", "parallel")),
    )(*args)


def _retention_constants():
    c = BLOCK
    hh = np.arange(N_RET_HEADS, dtype=np.float64)
    log_g = np.log1p(-(2.0 ** (-5.0 - hh)))
    j = np.arange(c, dtype=np.float64)
    rel = j[:, None] - j[None, :]
    scale = HEAD_W ** -0.5
    decay = np.where(rel >= 0, np.exp(log_g[:, None, None] * np.maximum(rel, 0.0)), 0.0) * scale
    q_decay = np.broadcast_to(np.exp(log_g[:, None] * (j + 1.0))[:, :, None], (N_RET_HEADS, c, HEAD_W))
    k_decay = np.broadcast_to((np.exp(log_g[:, None] * (c - 1.0 - j)) * scale)[:, :, None],
                              (N_RET_HEADS, c, HEAD_W))
    chunk_decay = tuple(float(v) for v in np.exp(log_g * c))
    return (jnp.asarray(decay, F32), jnp.asarray(q_decay, F32), jnp.asarray(k_decay, F32), chunk_decay)


def _retention_kernel(q_ref, k_ref, v_ref, gate_ref, dmat_ref, qd_ref, kd_ref, gn_ref, o_ref, state_ref,
                      *, tr, chunk_decay):
    i = pl.program_id(1)

    @pl.when(i == 0)
    def _():
        state_ref[...] = jnp.zeros_like(state_ref)

    for c in range(tr // BLOCK):
        rows = slice(c * BLOCK, (c + 1) * BLOCK)
        if c == 0:
            pos = i * tr + lax.broadcasted_iota(jnp.int32, (BLOCK, HEAD_W), 0)
            valid = pos >= N_PAD
        for h in range(N_RET_HEADS):
            cols = slice(h * HEAD_W, (h + 1) * HEAD_W)
            q = q_ref[0, rows, cols]
            k = k_ref[0, rows, cols]
            v = v_ref[0, rows, cols]
            if c == 0:
                k = jnp.where(valid, k, jnp.zeros_like(k))
                v = jnp.where(valid, v, jnp.zeros_like(v))
            scores = _dot_nt(q, k) * dmat_ref[h]
            inner = _dot(scores.astype(BF16), v)
            state = state_ref[h]
            cross = _dot(q, state.astype(BF16)) * qd_ref[h]
            k_scaled = (k.astype(F32) * kd_ref[h]).astype(BF16)
            state_ref[h] = chunk_decay[h] * state + _dot_tn(k_scaled, v)
            o = inner + cross
            oc = o - jnp.mean(o, axis=-1, keepdims=True)
            y = oc * lax.rsqrt(jnp.mean(oc * oc, axis=-1, keepdims=True) + EPS) * gn_ref[:, cols]
            gate = gate_ref[0, rows, cols].astype(F32)
            o_ref[0, rows, cols] = (y * (gate / (1.0 + jnp.exp(-gate)))).astype(BF16)


def _retention(proj, gn):
    b, p, _ = proj.shape
    w = N_RET_HEADS * HEAD_W
    tr = _pick_tile(p, (640, 512, 256, 128))
    dmat, qd, kd, chunk_decay = _retention_constants()

    def col_spec(cb):
        return pl.BlockSpec((1, tr, w), lambda bi, i: (bi, i, cb))

    cshape = (N_RET_HEADS, BLOCK, HEAD_W)
    return pl.pallas_call(
        functools.partial(_retention_kernel, tr=tr, chunk_decay=chunk_decay),
        grid=(b, p // tr),
        in_specs=[col_spec(0), col_spec(1), col_spec(2), col_spec(3),
                  _const_spec(cshape), _const_spec(cshape), _const_spec(cshape), _const_spec((1, w))],
        out_specs=pl.BlockSpec((1, tr, w), lambda bi, i: (bi, i, 0)),
        out_shape=jax.ShapeDtypeStruct((b, p, w), BF16),
        scratch_shapes=[pltpu.VMEM(cshape, F32)],
        compiler_params=_params(("parallel", "arbitrary")),
    )(proj, proj, proj, proj, dmat, qd, kd, gn.reshape(1, w))


def _diff_attn_kernel(slope_ref, lq1_ref, lk1_ref, lq2_ref, lk2_ref, q_ref, k_ref, v_ref, sel_ref, gn_ref, o_ref,
                      qs_ref, m_ref, l_ref, acc_ref, kmax_ref, *, tq, tk, rsub, lambda_init):
    h = pl.program_id(1)
    qi = pl.program_id(2)
    slope = slope_ref[h] * LOG2_E
    q0 = qi * tq
    n_sub = 2 * tq // rsub

    @pl.when(qi == 0)
    def _():
        kf = k_ref[0].astype(F32)
        norms = _dot((kf * kf).astype(BF16), sel_ref[...])
        kmax_ref[...] = jnp.sqrt(jnp.max(norms, axis=0, keepdims=True)) * 1.02

    q = q_ref[0]
    lane = lax.broadcasted_iota(jnp.int32, (tq, HEAD_W), 1)
    qs_ref[0:tq, :] = jnp.where(lane < DIFF_QK_DIM, q, jnp.zeros_like(q))
    qs_ref[tq:, :] = jnp.where(lane >= DIFF_QK_DIM, q, jnp.zeros_like(q))
    m_ref[...] = jnp.full_like(m_ref, MASK_VALUE)
    l_ref[...] = jnp.zeros_like(l_ref)
    acc_ref[...] = jnp.zeros_like(acc_ref)

    def step(j, mode):
        start = j * tk if isinstance(j, int) else pl.multiple_of(j * tk, tk)
        kpos = j * tk + lax.broadcasted_iota(jnp.int32, (1, tk), 1)
        bias = slope * (kpos - q0).astype(F32)

        def width(r):
            if mode != "diag":
                return tk
            last_row = (r * rsub) % tq + rsub
            return min(tk, -(-last_row // BLOCK) * BLOCK)

        def stage_a(r):
            w = width(r)
            k = k_ref[0, pl.ds(start, w), :]
            s = _dot_nt(qs_ref[r * rsub:(r + 1) * rsub, :], k) + bias[:, :w]
            if mode == "first":
                s = jnp.where(kpos >= N_PAD, s, MASK_VALUE)
            elif mode == "diag":
                qpos = q0 + (r * rsub) % tq + lax.broadcasted_iota(jnp.int32, (rsub, 1), 0)
                s = jnp.where((qpos >= kpos[:, :w]) & (kpos[:, :w] >= N_PAD), s, MASK_VALUE)
            return s, jnp.max(s, axis=1, keepdims=True)

        def stage_b(r, s, s_max):
            rows = slice(r * rsub, (r + 1) * rsub)
            m_prev = m_ref[rows, :]
            m_new = jnp.maximum(m_prev, s_max)
            p = jnp.exp2(s - m_new)
            alpha = jnp.exp2(m_prev - m_new)
            l_ref[rows, :] = alpha * l_ref[rows, :] + jnp.sum(p, axis=1, keepdims=True)
            m_ref[rows, :] = m_new
            return p.astype(BF16), alpha

        def stage_c(r, p, alpha):
            rows = slice(r * rsub, (r + 1) * rsub)
            v = v_ref[0, pl.ds(start, width(r)), :]
            acc_ref[rows, :] = alpha * acc_ref[rows, :] + _dot(p, v)

        a_out, b_out = {}, {}
        for t in range(n_sub + 2):
            if t < n_sub:
                a_out[t] = stage_a(t)
            if 0 <= t - 1 < n_sub:
                b_out[t - 1] = stage_b(t - 1, *a_out.pop(t - 1))
            if 0 <= t - 2 < n_sub:
                stage_c(t - 2, *b_out.pop(t - 2))

    step(qi, "diag")

    qf = qs_ref[...].astype(F32)
    qnorm = jnp.sqrt(jnp.sum(qf * qf, axis=1, keepdims=True))
    row = lax.broadcasted_iota(jnp.int32, (2 * tq, 1), 0)
    kmax = jnp.where(row < tq, kmax_ref[:, 0:1], kmax_ref[:, 1:2])
    slack = jnp.max(qnorm * kmax - m_ref[...])

    def needed(rel):
        return slack + slope * rel >= -ZERO_PROB_LOG2

    def cond(carry):
        j, rel = carry
        return (j >= 1) & needed(rel)

    def body(carry):
        j, rel = carry
        step(j, "plain")
        return j - 1, rel - tk

    j_end, rel_end = lax.while_loop(cond, body, (qi - 1, jnp.float32(-1.0)))

    @pl.when((j_end == 0) & needed(rel_end))
    def _():
        step(0, "first")

    o = acc_ref[...] / l_ref[...]
    lam = (jnp.exp(jnp.sum(lq1_ref[...] * lk1_ref[...], axis=1, keepdims=True))
           - jnp.exp(jnp.sum(lq2_ref[...] * lk2_ref[...], axis=1, keepdims=True)) + lambda_init)
    o = o[0:tq, :] - lam * o[tq:, :]
    o_ref[0] = (_rms(o, gn_ref[...]) * (1.0 - lambda_init)).astype(BF16)


def _diff_attention(proj, lam_q1, lam_k1, lam_q2, lam_k2, gn, lambda_init, col0):
    b, p, _ = proj.shape
    nh = N_DIFF_HEADS
    tq = tk = _pick_tile(p, (640, 512, 256, 128))
    rsub = 160 if tq % 160 == 0 else tq
    slopes = jnp.asarray(2.0 ** (-8.0 * (np.arange(nh) + 1.0) / nh), F32)
    lam_spec = _const_spec((1, DIFF_QK_DIM))
    kv_block = (1, p, HEAD_W)
    sel = np.zeros((HEAD_W, HEAD_W), np.float32)
    sel[:DIFF_QK_DIM, 0] = 1.0
    sel[DIFF_QK_DIM:, 1] = 1.0
    return pl.pallas_call(
        functools.partial(_diff_attn_kernel, tq=tq, tk=tk, rsub=rsub, lambda_init=lambda_init),
        grid=(b, nh, p // tq),
        in_specs=[pl.BlockSpec(memory_space=pltpu.SMEM),
                  lam_spec, lam_spec, lam_spec, lam_spec,
                  pl.BlockSpec((1, tq, HEAD_W), lambda bi, h, i: (bi, i, col0 + h)),
                  pl.BlockSpec(kv_block, lambda bi, h, i: (bi, 0, col0 + nh + h)),
                  pl.BlockSpec(kv_block, lambda bi, h, i: (bi, 0, col0 + 2 * nh + h)),
                  _const_spec((HEAD_W, HEAD_W)),
                  pl.BlockSpec((1, HEAD_W), lambda bi, h, i: (0, h))],
        out_specs=pl.BlockSpec((1, tq, HEAD_W), lambda bi, h, i: (bi, i, h)),
        out_shape=jax.ShapeDtypeStruct((b, p, nh * HEAD_W), BF16),
        scratch_shapes=[pltpu.VMEM((2 * tq, HEAD_W), BF16),
                        pltpu.VMEM((2 * tq, 1), F32),
                        pltpu.VMEM((2 * tq, 1), F32),
                        pltpu.VMEM((2 * tq, HEAD_W), F32),
                        pltpu.VMEM((1, HEAD_W), F32)],
        compiler_params=_params(("parallel", "parallel", "arbitrary")),
    )(slopes, lam_q1.reshape(1, -1), lam_k1.reshape(1, -1), lam_q2.reshape(1, -1), lam_k2.reshape(1, -1),
      proj, proj, proj, jnp.asarray(sel, BF16), gn.reshape(1, -1))


def _cumsum_blocks(tk):
    blocks, start = [], 0
    while start < tk:
        width = 256 if tk - start >= 256 else tk - start
        blocks.append((start, width))
        start += width
    return tuple(blocks)


def _sb_attn_kernel(q_ref, k_ref, v_ref, u_ref, o_ref, run_ref, acc_ref, *, tq):
    qi = pl.program_id(2)
    q0 = pl.multiple_of(qi * tq, BLOCK)

    def attend(pieces, kstart, masked, first):
        n = len(pieces)
        z, softplus, log_beta, mask, later, w = ([None] * n for _ in range(6))
        for i, (row_lo, n_rows, width) in enumerate(pieces):
            z[i] = _dot_nt(q_ref[0, row_lo:row_lo + n_rows, :], k_ref[0, pl.ds(kstart, width), :])
        for i, (row_lo, n_rows, width) in enumerate(pieces):
            neg_abs = lax.bitcast_convert_type(
                lax.bitcast_convert_type(z[i], jnp.uint32) | jnp.uint32(0x80000000), F32)
            sp = jnp.maximum(z[i], 0.0) + jnp.log(1.0 + jnp.exp2(neg_abs)) * INV_LN2
            log_beta[i] = z[i] - sp
            if masked:
                qpos = q0 + row_lo + lax.broadcasted_iota(jnp.int32, (n_rows, 1), 0)
                kpos = kstart + lax.broadcasted_iota(jnp.int32, (1, width), 1)
                mask[i] = (qpos > kpos) & (kpos >= N_PAD)
                sp = jnp.where(mask[i], sp, 0.0)
            softplus[i] = sp
        for i, (row_lo, n_rows, width) in enumerate(pieces):
            rows = slice(row_lo, row_lo + n_rows)
            run = jnp.zeros((n_rows, 1), F32) if first else run_ref[rows, :]
            sp16 = softplus[i].astype(BF16)
            blocks = _cumsum_blocks(width)
            parts = [None] * len(blocks)
            for idx in reversed(range(len(blocks))):
                b0, bw = blocks[idx]
                parts[idx] = _dot(sp16[:, b0:b0 + bw], u_ref[0:bw, 0:bw]) + run
                run = run - jnp.sum(softplus[i][:, b0:b0 + bw], axis=1, keepdims=True)
            run_ref[rows, :] = run
            later[i] = parts[0] if len(parts) == 1 else jnp.concatenate(parts, axis=1)
        for i in range(n):
            w[i] = jnp.exp2(log_beta[i] + later[i])
            if masked:
                w[i] = jnp.where(mask[i], w[i], 0.0)
            w[i] = w[i].astype(BF16)
        for i, (row_lo, n_rows, width) in enumerate(pieces):
            rows = slice(row_lo, row_lo + n_rows)
            contrib = _dot(w[i], v_ref[0, pl.ds(kstart, width), :])
            if first:
                acc_ref[rows, :] = contrib
            else:
                acc_ref[rows, :] += contrib

    attend([(r * BLOCK, BLOCK, (r + 1) * BLOCK) for r in range(tq // BLOCK)], q0, True, True)

    def walk(row_lo, n_rows):
        def more_needed():
            return jnp.max(run_ref[row_lo:row_lo + n_rows, :]) > -ZERO_PROB_LOG2

        def chunk(kstart, width, masked):
            kstart = pl.multiple_of(kstart, BLOCK)
            attend([(row_lo + r, BLOCK, width) for r in range(0, n_rows, BLOCK)], kstart, masked, False)

        def cond(carry):
            kstart, go = carry
            return (kstart >= BLOCK) & go

        def body(carry):
            kstart, _ = carry
            chunk(kstart, SB_KEY_CHUNK, False)
            return kstart - SB_KEY_CHUNK, more_needed()

        kstart, go = lax.while_loop(cond, body, (q0 - SB_KEY_CHUNK, more_needed()))

        def cond_tail(carry):
            kstart, go = carry
            return (kstart >= 0) & go

        def body_tail(carry):
            kstart, _ = carry
            chunk(kstart, BLOCK, True)
            return kstart - BLOCK, more_needed()

        lax.while_loop(cond_tail, body_tail, (kstart + SB_KEY_CHUNK - BLOCK, go))

    top = min(SB_TOP_ROWS, tq)
    walk(0, top)
    if top < tq:
        @pl.when(jnp.max(run_ref[top:, :]) > -ZERO_PROB_LOG2)
        def _():
            walk(top, tq - top)

    o_ref[0] = acc_ref[...].astype(BF16)


def _sb_attention(qkv):
    b, p, _ = qkv.shape
    nh = N_SB_HEADS
    tq = _pick_tile(p, (640, 512, 256, 128))
    umax = SB_KEY_CHUNK
    jj = np.arange(umax)
    u = jnp.asarray(-(jj[:, None] > jj[None, :]).astype(np.float32), BF16)
    kv_block = (1, p, HEAD_W)
    return pl.pallas_call(
        functools.partial(_sb_attn_kernel, tq=tq),
        grid=(b, nh, p // tq),
        in_specs=[pl.BlockSpec((1, tq, HEAD_W), lambda bi, h, i: (bi, i, h)),
                  pl.BlockSpec(kv_block, lambda bi, h, i: (bi, 0, nh + h)),
                  pl.BlockSpec(kv_block, lambda bi, h, i: (bi, 0, 2 * nh + h)),
                  _const_spec((umax, umax))],
        out_specs=pl.BlockSpec((1, tq, HEAD_W), lambda bi, h, i: (bi, i, h)),
        out_shape=jax.ShapeDtypeStruct((b, p, nh * HEAD_W), BF16),
        scratch_shapes=[pltpu.VMEM((tq, 1), F32), pltpu.VMEM((tq, HEAD_W), F32)],
        compiler_params=_params(("parallel", "parallel", "arbitrary")),
    )(qkv, qkv, qkv, u)


def kernel(x, meta_tokens, mix_norm, ffn_norm, ffn_up, ffn_conv, ffn_conv_b, ffn_down, ab_w_in, ab_ret_norm,
           ab_diff_norm, ab_lam_q1, ab_lam_k1, ab_lam_q2, ab_lam_k2, ab_w_out, c_w_in, c_w_out, final_norm):
    b, _, d = x.shape
    depth = mix_norm.shape[0]
    pad = jnp.zeros((b, N_PAD, d), x.dtype)
    meta = jnp.broadcast_to(meta_tokens[None].astype(x.dtype), (b, N_META, d))
    h = jnp.concatenate([pad, meta, x], axis=1)
    p = h.shape[1]
    t = b * p
    ret_w = N_RET_HEADS * HEAD_W
    for i in range(depth):
        if i % 2 == 0:
            e = i // 2
            lambda_init = 0.8 - 0.6 * math.exp(-0.3 * i)
            w_in = ab_w_in[e].astype(BF16)
            scales = (1.0, 1.0, 1.0, 1.0, DIFF_QK_DIM ** -0.5 * LOG2_E, 1.0, 1.0)
            proj = _norm_proj(h.reshape(t, d), mix_norm[i], w_in, scales, 512)
            proj = proj.reshape(b, p, -1)
            ret = _retention(proj, ab_ret_norm[e])
            dif = _diff_attention(proj, ab_lam_q1[e], ab_lam_k1[e], ab_lam_q2[e], ab_lam_k2[e],
                                  ab_diff_norm[e], lambda_init, 4 * N_RET_HEADS)
            w_out = ab_w_out[e].astype(BF16)
            h = _res_proj(h.reshape(t, d), [ret.reshape(t, -1), dif.reshape(t, -1)],
                          [w_out[:ret_w], w_out[ret_w:]]).reshape(b, p, d)
        else:
            o = i // 2
            w_in = c_w_in[o].astype(BF16)
            n_chunks = w_in.shape[1] // 512
            q_chunks = N_SB_HEADS * HEAD_W // 512
            scales = tuple(HEAD_W ** -0.5 * LOG2_E if c < q_chunks else 1.0 for c in range(n_chunks))
            qkv = _norm_proj(h.reshape(t, d), mix_norm[i], w_in, scales, 512).reshape(b, p, -1)
            att = _sb_attention(qkv)
            h = _res_proj(h.reshape(t, d), [att.reshape(t, -1)], [c_w_out[o].astype(BF16)]).reshape(b, p, d)
        h = _ffn(h, ffn_norm[i], ffn_up[i].astype(BF16), ffn_conv[i], ffn_conv_b[i], ffn_down[i].astype(BF16),
                 final_norm if i == depth - 1 else None)
    return h[:, BLOCK:, :]
```

```python
import functools
import math

import numpy as np
import jax
import jax.numpy as jnp
from jax import lax
from jax.experimental import pallas as pl
from jax.experimental.pallas import tpu as pltpu

F32 = jnp.float32
BF16 = jnp.bfloat16

N_META = 16
BLOCK = 128
N_PAD = BLOCK - N_META
EPS = 1e-6
MASK_VALUE = -1e30
HEAD_W = 128
N_RET_HEADS = 4
N_DIFF_HEADS = 4
DIFF_QK_DIM = 64
N_SB_HEADS = 8
CONV_WIDTH = 3
HALO = 16
LOG2_E = math.log2(math.e)
INV_LN2 = 1.0 / math.log(2.0)
ZERO_PROB_LOG2 = 160.0
SB_KEY_CHUNK = 256
SB_TOP_ROWS = 256

VMEM_LIMIT_BYTES = 56 * 1024 * 1024


def _pick_tile(n, candidates):
    for c in candidates:
        if n % c == 0:
            return c
    raise ValueError(f"no tile for {n} in {candidates}")


def _params(sem):
    return pltpu.CompilerParams(dimension_semantics=sem, vmem_limit_bytes=VMEM_LIMIT_BYTES)


def _const_spec(shape):
    return pl.BlockSpec(shape, lambda *_: (0,) * len(shape), pipeline_mode=pl.Buffered(1))


def _rms(x, g):
    return x * lax.rsqrt(jnp.mean(x * x, axis=-1, keepdims=True) + EPS) * g


def _dot(a, b):
    return jnp.dot(a, b, preferred_element_type=F32)


def _dot_nt(a, b):
    return lax.dot_general(a, b, (((1,), (1,)), ((), ())), preferred_element_type=F32)


def _dot_tn(a, b):
    return lax.dot_general(a, b, (((0,), (0,)), ((), ())), preferred_element_type=F32)


def _norm_proj_kernel(x_ref, g_ref, w_ref, o_ref, xn_ref, *, chunk, scales):
    xn_ref[...] = _rms(x_ref[...], g_ref[...]).astype(BF16)
    for c, scale in enumerate(scales):
        cols = slice(c * chunk, (c + 1) * chunk)
        r = _dot(xn_ref[...], w_ref[:, cols])
        if scale != 1.0:
            r = r * scale
        o_ref[:, cols] = r.astype(BF16)


def _norm_proj(x2, g, w, scales, chunk):
    t, d = x2.shape
    n = w.shape[1]
    tm = _pick_tile(t, (1024, 640, 512, 256, 128))
    return pl.pallas_call(
        functools.partial(_norm_proj_kernel, chunk=chunk, scales=scales),
        grid=(t // tm,),
        in_specs=[pl.BlockSpec((tm, d), lambda i: (i, 0)),
                  _const_spec((1, d)),
                  _const_spec((d, n))],
        out_specs=pl.BlockSpec((tm, n), lambda i: (i, 0)),
        out_shape=jax.ShapeDtypeStruct((t, n), BF16),
        scratch_shapes=[pltpu.VMEM((tm, d), BF16)],
        compiler_params=_params(("parallel",)),
    )(x2, g.reshape(1, d), w)


def _res_proj_kernel(*refs, n_in):
    h_ref, o_ref = refs[0], refs[-1]
    acc = h_ref[...]
    for a_ref, w_ref in zip(refs[1:1 + n_in], refs[1 + n_in:1 + 2 * n_in]):
        acc = acc + _dot(a_ref[...], w_ref[...])
    o_ref[...] = acc


def _res_proj(h2, acts, weights):
    t, d = h2.shape
    tm = _pick_tile(t, (1024, 640, 512, 256, 128))
    in_specs = [pl.BlockSpec((tm, d), lambda i: (i, 0))]
    in_specs += [pl.BlockSpec((tm, a.shape[1]), lambda i: (i, 0)) for a in acts]
    in_specs += [_const_spec(w.shape) for w in weights]
    return pl.pallas_call(
        functools.partial(_res_proj_kernel, n_in=len(acts)),
        grid=(t // tm,),
        in_specs=in_specs,
        out_specs=pl.BlockSpec((tm, d), lambda i: (i, 0)),
        out_shape=jax.ShapeDtypeStruct((t, d), F32),
        compiler_params=_params(("parallel",)),
    )(h2, *acts, *weights)


def _ffn_kernel(*refs, tm, f, fc, final):
    if final:
        x_ref, xh_ref, g_ref, wup_ref, wc_ref, bc_ref, wd_ref, fg_ref, o_ref, xn_ref, acc_ref = refs
    else:
        x_ref, xh_ref, g_ref, wup_ref, wc_ref, bc_ref, wd_ref, o_ref, xn_ref, acc_ref = refs
    i = pl.program_id(1)
    g = g_ref[...]
    xn_ref[0:HALO, :] = _rms(xh_ref[0], g).astype(BF16)
    xn_ref[HALO:, :] = _rms(x_ref[0], g).astype(BF16)
    pos = i * tm - HALO + lax.broadcasted_iota(jnp.int32, (tm + HALO, 1), 0)
    valid = pos >= N_PAD
    for c in range(f // fc):
        cols = slice(c * fc, (c + 1) * fc)
        gate = jnp.where(valid, _dot(xn_ref[...], wup_ref[:, cols]), 0.0)
        val = _dot(xn_ref[HALO:, :], wup_ref[:, f + c * fc:f + (c + 1) * fc])
        conv = bc_ref[:, cols]
        for tap in range(CONV_WIDTH):
            lo = HALO - (CONV_WIDTH - 1) + tap
            conv = conv + gate[lo:lo + tm, :] * wc_ref[tap:tap + 1, cols]
        mid = (conv / (1.0 + jnp.exp(-conv)) * val).astype(BF16)
        contrib = _dot(mid, wd_ref[cols, :])
        if c == 0:
            acc_ref[...] = contrib
        else:
            acc_ref[...] += contrib
    y = x_ref[0] + acc_ref[...]
    if final:
        y = _rms(y, fg_ref[...])
    o_ref[0] = y


def _ffn(h, g, w_up, w_conv, b_conv, w_down, final_g=None):
    b, p, d = h.shape
    f = w_down.shape[0]
    tm = _pick_tile(p, (832, 640, 512, 256, 128))
    fc = 256
    final = final_g is not None
    in_specs = [pl.BlockSpec((1, tm, d), lambda bi, i: (bi, i, 0)),
                pl.BlockSpec((1, HALO, d),
                             lambda bi, i: (bi, jnp.maximum(i * (tm // HALO) - 1, 0), 0)),
                _const_spec((1, d)),
                _const_spec((d, 2 * f)),
                _const_spec((CONV_WIDTH, f)),
                _const_spec((1, f)),
                _const_spec((f, d))]
    args = [h, h, g.reshape(1, d), w_up, w_conv, b_conv.reshape(1, f), w_down]
    if final:
        in_specs.append(_const_spec((1, d)))
        args.append(final_g.reshape(1, d))
    return pl.pallas_call(
        functools.partial(_ffn_kernel, tm=tm, f=f, fc=fc, final=final),
        grid=(b, p // tm),
        in_specs=in_specs,
        out_specs=pl.BlockSpec((1, tm, d), lambda bi, i: (bi, i, 0)),
        out_shape=jax.ShapeDtypeStruct((b, p, d), F32),
        scratch_shapes=[pltpu.VMEM((tm + HALO, d), BF16), pltpu.VMEM((tm, d), F32)],
        compiler_params=_params(("parallel", "parallel")),
    )(*args)


def _retention_constants():
    c = BLOCK
    hh = np.arange(N_RET_HEADS, dtype=np.float64)
    log_g = np.log1p(-(2.0 ** (-5.0 - hh)))
    j = np.arange(c, dtype=np.float64)
    rel = j[:, None] - j[None, :]
    scale = HEAD_W ** -0.5
    decay = np.where(rel >= 0, np.exp(log_g[:, None, None] * np.maximum(rel, 0.0)), 0.0) * scale
    q_decay = np.broadcast_to(np.exp(log_g[:, None] * (j + 1.0))[:, :, None], (N_RET_HEADS, c, HEAD_W))
    k_decay = np.broadcast_to((np.exp(log_g[:, None] * (c - 1.0 - j)) * scale)[:, :, None],
                              (N_RET_HEADS, c, HEAD_W))
    chunk_decay = tuple(float(v) for v in np.exp(log_g * c))
    return (jnp.asarray(decay, F32), jnp.asarray(q_decay, F32), jnp.asarray(k_decay, F32), chunk_decay)


def _retention_kernel(q_ref, k_ref, v_ref, gate_ref, dmat_ref, qd_ref, kd_ref, gn_ref, o_ref, state_ref,
                      *, tr, chunk_decay):
    i = pl.program_id(1)

    @pl.when(i == 0)
    def _():
        state_ref[...] = jnp.zeros_like(state_ref)

    for c in range(tr // BLOCK):
        rows = slice(c * BLOCK, (c + 1) * BLOCK)
        if c == 0:
            pos = i * tr + lax.broadcasted_iota(jnp.int32, (BLOCK, HEAD_W), 0)
            valid = pos >= N_PAD
        for h in range(N_RET_HEADS):
            cols = slice(h * HEAD_W, (h + 1) * HEAD_W)
            q = q_ref[0, rows, cols]
            k = k_ref[0, rows, cols]
            v = v_ref[0, rows, cols]
            if c == 0:
                k = jnp.where(valid, k, jnp.zeros_like(k))
                v = jnp.where(valid, v, jnp.zeros_like(v))
            scores = _dot_nt(q, k) * dmat_ref[h]
            inner = _dot(scores.astype(BF16), v)
            state = state_ref[h]
            cross = _dot(q, state.astype(BF16)) * qd_ref[h]
            k_scaled = (k.astype(F32) * kd_ref[h]).astype(BF16)
            state_ref[h] = chunk_decay[h] * state + _dot_tn(k_scaled, v)
            o = inner + cross
            oc = o - jnp.mean(o, axis=-1, keepdims=True)
            y = oc * lax.rsqrt(jnp.mean(oc * oc, axis=-1, keepdims=True) + EPS) * gn_ref[:, cols]
            gate = gate_ref[0, rows, cols].astype(F32)
            o_ref[0, rows, cols] = (y * (gate / (1.0 + jnp.exp(-gate)))).astype(BF16)


def _retention(proj, gn):
    b, p, _ = proj.shape
    w = N_RET_HEADS * HEAD_W
    tr = _pick_tile(p, (640, 512, 256, 128))
    dmat, qd, kd, chunk_decay = _retention_constants()

    def col_spec(cb):
        return pl.BlockSpec((1, tr, w), lambda bi, i: (bi, i, cb))

    cshape = (N_RET_HEADS, BLOCK, HEAD_W)
    return pl.pallas_call(
        functools.partial(_retention_kernel, tr=tr, chunk_decay=chunk_decay),
        grid=(b, p // tr),
        in_specs=[col_spec(0), col_spec(1), col_spec(2), col_spec(3),
                  _const_spec(cshape), _const_spec(cshape), _const_spec(cshape), _const_spec((1, w))],
        out_specs=pl.BlockSpec((1, tr, w), lambda bi, i: (bi, i, 0)),
        out_shape=jax.ShapeDtypeStruct((b, p, w), BF16),
        scratch_shapes=[pltpu.VMEM(cshape, F32)],
        compiler_params=_params(("parallel", "arbitrary")),
    )(proj, proj, proj, proj, dmat, qd, kd, gn.reshape(1, w))


def _diff_attn_kernel(slope_ref, lq1_ref, lk1_ref, lq2_ref, lk2_ref, q_ref, k_ref, v_ref, sel_ref, gn_ref, o_ref,
                      qs_ref, m_ref, l_ref, acc_ref, kmax_ref, s0_ref, s1_ref, *, tq, tk, rsub, lambda_init):
    h = pl.program_id(1)
    qi = pl.program_id(2)
    slope = slope_ref[h] * LOG2_E
    q0 = qi * tq
    n_sub = 2 * tq // rsub

    @pl.when(qi == 0)
    def _():
        kf = k_ref[0].astype(F32)
        norms = _dot((kf * kf).astype(BF16), sel_ref[...])
        kmax_ref[...] = jnp.sqrt(jnp.max(norms, axis=0, keepdims=True)) * 1.02

    q = q_ref[0]
    lane = lax.broadcasted_iota(jnp.int32, (tq, HEAD_W), 1)
    qs_ref[0:tq, :] = jnp.where(lane < DIFF_QK_DIM, q, jnp.zeros_like(q))
    qs_ref[tq:, :] = jnp.where(lane >= DIFF_QK_DIM, q, jnp.zeros_like(q))
    m_ref[...] = jnp.full_like(m_ref, MASK_VALUE)
    l_ref[...] = jnp.zeros_like(l_ref)
    acc_ref[...] = jnp.zeros_like(acc_ref)

    def tile_start(j):
        return j * tk if isinstance(j, int) else pl.multiple_of(j * tk, tk)

    s_refs = (s0_ref, s1_ref)

    def issue_scores(j, slot):
        s_refs[slot][...] = _dot_nt(qs_ref[...], k_ref[0, pl.ds(tile_start(j), tk), :])

    def consume(j, slot, mode, prefetch):
        s_ref = s_refs[slot]
        if prefetch is not None:
            issue_scores(prefetch, 1 - slot)
        start = tile_start(j)
        kpos = j * tk + lax.broadcasted_iota(jnp.int32, (1, tk), 1)
        bias = slope * (kpos - q0).astype(F32)
        v = v_ref[0, pl.ds(start, tk), :]

        def stage_b(r):
            rows = slice(r * rsub, (r + 1) * rsub)
            s = s_ref[rows, :] + bias
            if mode == "first":
                s = jnp.where(kpos >= N_PAD, s, MASK_VALUE)
            elif mode == "diag":
                qpos = q0 + (r * rsub) % tq + lax.broadcasted_iota(jnp.int32, (rsub, 1), 0)
                s = jnp.where((qpos >= kpos) & (kpos >= N_PAD), s, MASK_VALUE)
            m_prev = m_ref[rows, :]
            m_new = jnp.maximum(m_prev, jnp.max(s, axis=1, keepdims=True))
            p = jnp.exp2(s - m_new)
            alpha = jnp.exp2(m_prev - m_new)
            l_ref[rows, :] = alpha * l_ref[rows, :] + jnp.sum(p, axis=1, keepdims=True)
            m_ref[rows, :] = m_new
            return p.astype(BF16), alpha

        def stage_c(r, p, alpha):
            rows = slice(r * rsub, (r + 1) * rsub)
            acc_ref[rows, :] = alpha * acc_ref[rows, :] + _dot(p, v)

        b_out = {}
        for t in range(n_sub + 1):
            if t < n_sub:
                b_out[t] = stage_b(t)
            if t >= 1:
                stage_c(t - 1, *b_out.pop(t - 1))

    issue_scores(qi, 0)
    consume(qi, 0, "diag", jnp.maximum(qi - 1, 0))

    qf = qs_ref[...].astype(F32)
    qnorm = jnp.sqrt(jnp.sum(qf * qf, axis=1, keepdims=True))
    row = lax.broadcasted_iota(jnp.int32, (2 * tq, 1), 0)
    kmax = jnp.where(row < tq, kmax_ref[:, 0:1], kmax_ref[:, 1:2])
    slack = jnp.max(qnorm * kmax - m_ref[...])

    def needed(rel):
        return slack + slope * rel >= -ZERO_PROB_LOG2

    def cond(carry):
        j, rel, _ = carry
        return (j >= 1) & needed(rel)

    def body(carry):
        j, rel, slot = carry
        for static_slot in (0, 1):
            @pl.when(slot == static_slot)
            def _():
                consume(j, static_slot, "plain", j - 1)
        return j - 1, rel - tk, 1 - slot

    j_end, rel_end, slot_end = lax.while_loop(cond, body, (qi - 1, jnp.float32(-1.0), jnp.int32(1)))

    for static_slot in (0, 1):
        @pl.when((j_end == 0) & needed(rel_end) & (slot_end == static_slot))
        def _():
            consume(0, static_slot, "first", None)

    o = acc_ref[...] / l_ref[...]
    lam = (jnp.exp(jnp.sum(lq1_ref[...] * lk1_ref[...], axis=1, keepdims=True))
           - jnp.exp(jnp.sum(lq2_ref[...] * lk2_ref[...], axis=1, keepdims=True)) + lambda_init)
    o = o[0:tq, :] - lam * o[tq:, :]
    o_ref[0] = (_rms(o, gn_ref[...]) * (1.0 - lambda_init)).astype(BF16)


def _diff_attention(proj, lam_q1, lam_k1, lam_q2, lam_k2, gn, lambda_init, col0):
    b, p, _ = proj.shape
    nh = N_DIFF_HEADS
    tq = tk = _pick_tile(p, (640, 512, 256, 128))
    rsub = 320 if tq % 320 == 0 else tq
    slopes = jnp.asarray(2.0 ** (-8.0 * (np.arange(nh) + 1.0) / nh), F32)
    lam_spec = _const_spec((1, DIFF_QK_DIM))
    kv_block = (1, p, HEAD_W)
    sel = np.zeros((HEAD_W, HEAD_W), np.float32)
    sel[:DIFF_QK_DIM, 0] = 1.0
    sel[DIFF_QK_DIM:, 1] = 1.0
    return pl.pallas_call(
        functools.partial(_diff_attn_kernel, tq=tq, tk=tk, rsub=rsub, lambda_init=lambda_init),
        grid=(b, nh, p // tq),
        in_specs=[pl.BlockSpec(memory_space=pltpu.SMEM),
                  lam_spec, lam_spec, lam_spec, lam_spec,
                  pl.BlockSpec((1, tq, HEAD_W), lambda bi, h, i: (bi, i, col0 + h)),
                  pl.BlockSpec(kv_block, lambda bi, h, i: (bi, 0, col0 + nh + h)),
                  pl.BlockSpec(kv_block, lambda bi, h, i: (bi, 0, col0 + 2 * nh + h)),
                  _const_spec((HEAD_W, HEAD_W)),
                  pl.BlockSpec((1, HEAD_W), lambda bi, h, i: (0, h))],
        out_specs=pl.BlockSpec((1, tq, HEAD_W), lambda bi, h, i: (bi, i, h)),
        out_shape=jax.ShapeDtypeStruct((b, p, nh * HEAD_W), BF16),
        scratch_shapes=[pltpu.VMEM((2 * tq, HEAD_W), BF16),
                        pltpu.VMEM((2 * tq, 1), F32),
                        pltpu.VMEM((2 * tq, 1), F32),
                        pltpu.VMEM((2 * tq, HEAD_W), F32),
                        pltpu.VMEM((1, HEAD_W), F32),
                        pltpu.VMEM((2 * tq, tk), F32),
                        pltpu.VMEM((2 * tq, tk), F32)],
        compiler_params=_params(("parallel", "parallel", "arbitrary")),
    )(slopes, lam_q1.reshape(1, -1), lam_k1.reshape(1, -1), lam_q2.reshape(1, -1), lam_k2.reshape(1, -1),
      proj, proj, proj, jnp.asarray(sel, BF16), gn.reshape(1, -1))


def _cumsum_blocks(tk):
    blocks, start = [], 0
    while start < tk:
        width = 256 if tk - start >= 256 else tk - start
        blocks.append((start, width))
        start += width
    return tuple(blocks)


def _sb_attn_kernel(q_ref, k_ref, v_ref, u_ref, o_ref, run_ref, acc_ref, *, tq):
    qi = pl.program_id(2)
    q0 = pl.multiple_of(qi * tq, BLOCK)

    def attend(pieces, kstart, masked, first):
        n = len(pieces)
        z, softplus, log_beta, mask, later, w = ([None] * n for _ in range(6))
        for i, (row_lo, n_rows, width) in enumerate(pieces):
            z[i] = _dot_nt(q_ref[0, row_lo:row_lo + n_rows, :], k_ref[0, pl.ds(kstart, width), :])
        for i, (row_lo, n_rows, width) in enumerate(pieces):
            neg_abs = lax.bitcast_convert_type(
                lax.bitcast_convert_type(z[i], jnp.uint32) | jnp.uint32(0x80000000), F32)
            sp = jnp.maximum(z[i], 0.0) + jnp.log(1.0 + jnp.exp2(neg_abs)) * INV_LN2
            log_beta[i] = z[i] - sp
            if masked:
                qpos = q0 + row_lo + lax.broadcasted_iota(jnp.int32, (n_rows, 1), 0)
                kpos = kstart + lax.broadcasted_iota(jnp.int32, (1, width), 1)
                mask[i] = (qpos > kpos) & (kpos >= N_PAD)
                sp = jnp.where(mask[i], sp, 0.0)
            softplus[i] = sp
        for i, (row_lo, n_rows, width) in enumerate(pieces):
            rows = slice(row_lo, row_lo + n_rows)
            run = jnp.zeros((n_rows, 1), F32) if first else run_ref[rows, :]
            sp16 = softplus[i].astype(BF16)
            blocks = _cumsum_blocks(width)
            parts = [None] * len(blocks)
            for idx in reversed(range(len(blocks))):
                b0, bw = blocks[idx]
                parts[idx] = _dot(sp16[:, b0:b0 + bw], u_ref[0:bw, 0:bw]) + run
                run = run - jnp.sum(softplus[i][:, b0:b0 + bw], axis=1, keepdims=True)
            run_ref[rows, :] = run
            later[i] = parts[0] if len(parts) == 1 else jnp.concatenate(parts, axis=1)
        for i in range(n):
            w[i] = jnp.exp2(log_beta[i] + later[i])
            if masked:
                w[i] = jnp.where(mask[i], w[i], 0.0)
            w[i] = w[i].astype(BF16)
        for i, (row_lo, n_rows, width) in enumerate(pieces):
            rows = slice(row_lo, row_lo + n_rows)
            contrib = _dot(w[i], v_ref[0, pl.ds(kstart, width), :])
            if first:
                acc_ref[rows, :] = contrib
            else:
                acc_ref[rows, :] += contrib

    attend([(r * BLOCK, BLOCK, (r + 1) * BLOCK) for r in range(tq // BLOCK)], q0, True, True)

    def walk(row_lo, n_rows):
        def more_needed():
            return jnp.max(run_ref[row_lo:row_lo + n_rows, :]) > -ZERO_PROB_LOG2

        def chunk(kstart, width, masked):
            kstart = pl.multiple_of(kstart, BLOCK)
            attend([(row_lo + r, BLOCK, width) for r in range(0, n_rows, BLOCK)], kstart, masked, False)

        def cond(carry):
            kstart, go = carry
            return (kstart >= BLOCK) & go

        def body(carry):
            kstart, _ = carry
            chunk(kstart, SB_KEY_CHUNK, False)
            return kstart - SB_KEY_CHUNK, more_needed()

        kstart, go = lax.while_loop(cond, body, (q0 - SB_KEY_CHUNK, more_needed()))

        def cond_tail(carry):
            kstart, go = carry
            return (kstart >= 0) & go

        def body_tail(carry):
            kstart, _ = carry
            chunk(kstart, BLOCK, True)
            return kstart - BLOCK, more_needed()

        lax.while_loop(cond_tail, body_tail, (kstart + SB_KEY_CHUNK - BLOCK, go))

    top = min(SB_TOP_ROWS, tq)
    walk(0, top)
    if top < tq:
        @pl.when(jnp.max(run_ref[top:, :]) > -ZERO_PROB_LOG2)
        def _():
            walk(top, tq - top)

    o_ref[0] = acc_ref[...].astype(BF16)


def _sb_attention(qkv):
    b, p, _ = qkv.shape
    nh = N_SB_HEADS
    tq = _pick_tile(p, (640, 512, 256, 128))
    umax = SB_KEY_CHUNK
    jj = np.arange(umax)
    u = jnp.asarray(-(jj[:, None] > jj[None, :]).astype(np.float32), BF16)
    kv_block = (1, p, HEAD_W)
    return pl.pallas_call(
        functools.partial(_sb_attn_kernel, tq=tq),
        grid=(b, nh, p // tq),
        in_specs=[pl.BlockSpec((1, tq, HEAD_W), lambda bi, h, i: (bi, i, h)),
                  pl.BlockSpec(kv_block, lambda bi, h, i: (bi, 0, nh + h)),
                  pl.BlockSpec(kv_block, lambda bi, h, i: (bi, 0, 2 * nh + h)),
                  _const_spec((umax, umax))],
        out_specs=pl.BlockSpec((1, tq, HEAD_W), lambda bi, h, i: (bi, i, h)),
        out_shape=jax.ShapeDtypeStruct((b, p, nh * HEAD_W), BF16),
        scratch_shapes=[pltpu.VMEM((tq, 1), F32), pltpu.VMEM((tq, HEAD_W), F32)],
        compiler_params=_params(("parallel", "parallel", "arbitrary")),
    )(qkv, qkv, qkv, u)


def kernel(x, meta_tokens, mix_norm, ffn_norm, ffn_up, ffn_conv, ffn_conv_b, ffn_down, ab_w_in, ab_ret_norm,
           ab_diff_norm, ab_lam_q1, ab_lam_k1, ab_lam_q2, ab_lam_k2, ab_w_out, c_w_in, c_w_out, final_norm):
    b, _, d = x.shape
    depth = mix_norm.shape[0]
    pad = jnp.zeros((b, N_PAD, d), x.dtype)
    meta = jnp.broadcast_to(meta_tokens[None].astype(x.dtype), (b, N_META, d))
    h = jnp.concatenate([pad, meta, x], axis=1)
    p = h.shape[1]
    t = b * p
    ret_w = N_RET_HEADS * HEAD_W
    for i in range(depth):
        if i % 2 == 0:
            e = i // 2
            lambda_init = 0.8 - 0.6 * math.exp(-0.3 * i)
            w_in = ab_w_in[e].astype(BF16)
            scales = (1.0, 1.0, 1.0, 1.0, DIFF_QK_DIM ** -0.5 * LOG2_E, 1.0, 1.0)
            proj = _norm_proj(h.reshape(t, d), mix_norm[i], w_in, scales, 512)
            proj = proj.reshape(b, p, -1)
            ret = _retention(proj, ab_ret_norm[e])
            dif = _diff_attention(proj, ab_lam_q1[e], ab_lam_k1[e], ab_lam_q2[e], ab_lam_k2[e],
                                  ab_diff_norm[e], lambda_init, 4 * N_RET_HEADS)
            w_out = ab_w_out[e].astype(BF16)
            h = _res_proj(h.reshape(t, d), [ret.reshape(t, -1), dif.reshape(t, -1)],
                          [w_out[:ret_w], w_out[ret_w:]]).reshape(b, p, d)
        else:
            o = i // 2
            w_in = c_w_in[o].astype(BF16)
            n_chunks = w_in.shape[1] // 512
            q_chunks = N_SB_HEADS * HEAD_W // 512
            scales = tuple(HEAD_W ** -0.5 * LOG2_E if c < q_chunks else 1.0 for c in range(n_chunks))
            qkv = _norm_proj(h.reshape(t, d), mix_norm[i], w_in, scales, 512).reshape(b, p, -1)
            att = _sb_attention(qkv)
            h = _res_proj(h.reshape(t, d), [att.reshape(t, -1)], [c_w_out[o].astype(BF16)]).reshape(b, p, d)
        h = _ffn(h, ffn_norm[i], ffn_up[i].astype(BF16), ffn_conv[i], ffn_conv_b[i], ffn_down[i].astype(BF16),
                 final_norm if i == depth - 1 else None)
    return h[:, BLOCK:, :]
```

```python
import functools
import math

import numpy as np
import jax
import jax.numpy as jnp
from jax import lax
from jax.experimental import pallas as pl
from jax.experimental.pallas import tpu as pltpu

F32 = jnp.float32
BF16 = jnp.bfloat16

N_META = 16
BLOCK = 128
N_PAD = BLOCK - N_META
EPS = 1e-6
MASK_VALUE = -1e30
HEAD_W = 128
N_RET_HEADS = 4
N_DIFF_HEADS = 4
DIFF_QK_DIM = 64
N_SB_HEADS = 8
CONV_WIDTH = 3
HALO = 16
LOG2_E = math.log2(math.e)
INV_LN2 = 1.0 / math.log(2.0)
ZERO_PROB_LOG2 = 160.0
SB_KEY_CHUNK = 256
SB_TOP_ROWS = 256

VMEM_LIMIT_BYTES = 56 * 1024 * 1024


def _pick_tile(n, candidates):
    for c in candidates:
        if n % c == 0:
            return c
    raise ValueError(f"no tile for {n} in {candidates}")


def _params(sem):
    return pltpu.CompilerParams(dimension_semantics=sem, vmem_limit_bytes=VMEM_LIMIT_BYTES)


def _const_spec(shape):
    return pl.BlockSpec(shape, lambda *_: (0,) * len(shape), pipeline_mode=pl.Buffered(1))


def _rms(x, g):
    return x * lax.rsqrt(jnp.mean(x * x, axis=-1, keepdims=True) + EPS) * g


def _dot(a, b):
    return jnp.dot(a, b, preferred_element_type=F32)


def _dot_nt(a, b):
    return lax.dot_general(a, b, (((1,), (1,)), ((), ())), preferred_element_type=F32)


def _dot_tn(a, b):
    return lax.dot_general(a, b, (((0,), (0,)), ((), ())), preferred_element_type=F32)


def _norm_proj_kernel(x_ref, g_ref, w_ref, o_ref, xn_ref, *, chunk, scales):
    xn_ref[...] = _rms(x_ref[...], g_ref[...]).astype(BF16)
    for c, scale in enumerate(scales):
        cols = slice(c * chunk, (c + 1) * chunk)
        r = _dot(xn_ref[...], w_ref[:, cols])
        if scale != 1.0:
            r = r * scale
        o_ref[:, cols] = r.astype(BF16)


def _norm_proj(x2, g, w, scales, chunk):
    t, d = x2.shape
    n = w.shape[1]
    tm = _pick_tile(t, (1024, 640, 512, 256, 128))
    return pl.pallas_call(
        functools.partial(_norm_proj_kernel, chunk=chunk, scales=scales),
        grid=(t // tm,),
        in_specs=[pl.BlockSpec((tm, d), lambda i: (i, 0)),
                  _const_spec((1, d)),
                  _const_spec((d, n))],
        out_specs=pl.BlockSpec((tm, n), lambda i: (i, 0)),
        out_shape=jax.ShapeDtypeStruct((t, n), BF16),
        scratch_shapes=[pltpu.VMEM((tm, d), BF16)],
        compiler_params=_params(("parallel",)),
    )(x2, g.reshape(1, d), w)


def _res_proj_kernel(*refs, n_in):
    h_ref, o_ref = refs[0], refs[-1]
    acc = h_ref[...]
    for a_ref, w_ref in zip(refs[1:1 + n_in], refs[1 + n_in:1 + 2 * n_in]):
        acc = acc + _dot(a_ref[...], w_ref[...])
    o_ref[...] = acc


def _res_proj(h2, acts, weights):
    t, d = h2.shape
    tm = _pick_tile(t, (1024, 640, 512, 256, 128))
    in_specs = [pl.BlockSpec((tm, d), lambda i: (i, 0))]
    in_specs += [pl.BlockSpec((tm, a.shape[1]), lambda i: (i, 0)) for a in acts]
    in_specs += [_const_spec(w.shape) for w in weights]
    return pl.pallas_call(
        functools.partial(_res_proj_kernel, n_in=len(acts)),
        grid=(t // tm,),
        in_specs=in_specs,
        out_specs=pl.BlockSpec((tm, d), lambda i: (i, 0)),
        out_shape=jax.ShapeDtypeStruct((t, d), F32),
        compiler_params=_params(("parallel",)),
    )(h2, *acts, *weights)


def _ffn_kernel(*refs, tm, f, fc, final):
    if final:
        x_ref, xh_ref, g_ref, wup_ref, wc_ref, bc_ref, wd_ref, fg_ref, o_ref, xn_ref, acc_ref = refs
    else:
        x_ref, xh_ref, g_ref, wup_ref, wc_ref, bc_ref, wd_ref, o_ref, xn_ref, acc_ref = refs
    i = pl.program_id(1)
    g = g_ref[...]
    xn_ref[0:HALO, :] = _rms(xh_ref[0], g).astype(BF16)
    xn_ref[HALO:, :] = _rms(x_ref[0], g).astype(BF16)
    pos = i * tm - HALO + lax.broadcasted_iota(jnp.int32, (tm + HALO, 1), 0)
    valid = pos >= N_PAD
    def up_proj(c):
        gate = _dot(xn_ref[...], wup_ref[:, c * fc:(c + 1) * fc])
        val = _dot(xn_ref[HALO:, :], wup_ref[:, f + c * fc:f + (c + 1) * fc])
        return gate, val

    n_chunks = f // fc
    nxt = up_proj(0)
    for c in range(n_chunks):
        cols = slice(c * fc, (c + 1) * fc)
        gate, val = nxt
        if c + 1 < n_chunks:
            nxt = up_proj(c + 1)
        gate = jnp.where(valid, gate, 0.0)
        conv = bc_ref[:, cols]
        for tap in range(CONV_WIDTH):
            lo = HALO - (CONV_WIDTH - 1) + tap
            conv = conv + gate[lo:lo + tm, :] * wc_ref[tap:tap + 1, cols]
        mid = (conv / (1.0 + jnp.exp(-conv)) * val).astype(BF16)
        contrib = _dot(mid, wd_ref[cols, :])
        if c == 0:
            acc_ref[...] = contrib
        else:
            acc_ref[...] += contrib
    y = x_ref[0] + acc_ref[...]
    if final:
        y = _rms(y, fg_ref[...])
    o_ref[0] = y


def _ffn(h, g, w_up, w_conv, b_conv, w_down, final_g=None):
    b, p, d = h.shape
    f = w_down.shape[0]
    tm = _pick_tile(p, (832, 640, 512, 256, 128))
    fc = 256
    final = final_g is not None
    in_specs = [pl.BlockSpec((1, tm, d), lambda bi, i: (bi, i, 0)),
                pl.BlockSpec((1, HALO, d),
                             lambda bi, i: (bi, jnp.maximum(i * (tm // HALO) - 1, 0), 0)),
                _const_spec((1, d)),
                _const_spec((d, 2 * f)),
                _const_spec((CONV_WIDTH, f)),
                _const_spec((1, f)),
                _const_spec((f, d))]
    args = [h, h, g.reshape(1, d), w_up, w_conv, b_conv.reshape(1, f), w_down]
    if final:
        in_specs.append(_const_spec((1, d)))
        args.append(final_g.reshape(1, d))
    return pl.pallas_call(
        functools.partial(_ffn_kernel, tm=tm, f=f, fc=fc, final=final),
        grid=(b, p // tm),
        in_specs=in_specs,
        out_specs=pl.BlockSpec((1, tm, d), lambda bi, i: (bi, i, 0)),
        out_shape=jax.ShapeDtypeStruct((b, p, d), F32),
        scratch_shapes=[pltpu.VMEM((tm + HALO, d), BF16), pltpu.VMEM((tm, d), F32)],
        compiler_params=_params(("parallel", "parallel")),
    )(*args)


def _retention_constants():
    c = BLOCK
    hh = np.arange(N_RET_HEADS, dtype=np.float64)
    log_g = np.log1p(-(2.0 ** (-5.0 - hh)))
    j = np.arange(c, dtype=np.float64)
    rel = j[:, None] - j[None, :]
    scale = HEAD_W ** -0.5
    decay = np.where(rel >= 0, np.exp(log_g[:, None, None] * np.maximum(rel, 0.0)), 0.0) * scale
    q_decay = np.broadcast_to(np.exp(log_g[:, None] * (j + 1.0))[:, :, None], (N_RET_HEADS, c, HEAD_W))
    k_decay = np.broadcast_to((np.exp(log_g[:, None] * (c - 1.0 - j)) * scale)[:, :, None],
                              (N_RET_HEADS, c, HEAD_W))
    chunk_decay = tuple(float(v) for v in np.exp(log_g * c))
    return (jnp.asarray(decay, F32), jnp.asarray(q_decay, F32), jnp.asarray(k_decay, F32), chunk_decay)


def _retention_kernel(q_ref, k_ref, v_ref, gate_ref, dmat_ref, qd_ref, kd_ref, gn_ref, o_ref, state_ref,
                      *, tr, chunk_decay):
    i = pl.program_id(1)

    @pl.when(i == 0)
    def _():
        state_ref[...] = jnp.zeros_like(state_ref)

    for c in range(tr // BLOCK):
        rows = slice(c * BLOCK, (c + 1) * BLOCK)
        if c == 0:
            pos = i * tr + lax.broadcasted_iota(jnp.int32, (BLOCK, HEAD_W), 0)
            valid = pos >= N_PAD
        for h in range(N_RET_HEADS):
            cols = slice(h * HEAD_W, (h + 1) * HEAD_W)
            q = q_ref[0, rows, cols]
            k = k_ref[0, rows, cols]
            v = v_ref[0, rows, cols]
            if c == 0:
                k = jnp.where(valid, k, jnp.zeros_like(k))
                v = jnp.where(valid, v, jnp.zeros_like(v))
            scores = _dot_nt(q, k) * dmat_ref[h]
            inner = _dot(scores.astype(BF16), v)
            state = state_ref[h]
            cross = _dot(q, state.astype(BF16)) * qd_ref[h]
            k_scaled = (k.astype(F32) * kd_ref[h]).astype(BF16)
            state_ref[h] = chunk_decay[h] * state + _dot_tn(k_scaled, v)
            o = inner + cross
            oc = o - jnp.mean(o, axis=-1, keepdims=True)
            y = oc * lax.rsqrt(jnp.mean(oc * oc, axis=-1, keepdims=True) + EPS) * gn_ref[:, cols]
            gate = gate_ref[0, rows, cols].astype(F32)
            o_ref[0, rows, cols] = (y * (gate / (1.0 + jnp.exp(-gate)))).astype(BF16)


def _retention(proj, gn):
    b, p, _ = proj.shape
    w = N_RET_HEADS * HEAD_W
    tr = _pick_tile(p, (640, 512, 256, 128))
    dmat, qd, kd, chunk_decay = _retention_constants()

    def col_spec(cb):
        return pl.BlockSpec((1, tr, w), lambda bi, i: (bi, i, cb))

    cshape = (N_RET_HEADS, BLOCK, HEAD_W)
    return pl.pallas_call(
        functools.partial(_retention_kernel, tr=tr, chunk_decay=chunk_decay),
        grid=(b, p // tr),
        in_specs=[col_spec(0), col_spec(1), col_spec(2), col_spec(3),
                  _const_spec(cshape), _const_spec(cshape), _const_spec(cshape), _const_spec((1, w))],
        out_specs=pl.BlockSpec((1, tr, w), lambda bi, i: (bi, i, 0)),
        out_shape=jax.ShapeDtypeStruct((b, p, w), BF16),
        scratch_shapes=[pltpu.VMEM(cshape, F32)],
        compiler_params=_params(("parallel", "arbitrary")),
    )(proj, proj, proj, proj, dmat, qd, kd, gn.reshape(1, w))


def _diff_attn_kernel(slope_ref, lq1_ref, lk1_ref, lq2_ref, lk2_ref, q_ref, k_ref, v_ref, sel_ref, gn_ref, o_ref,
                      qs_ref, m_ref, l_ref, acc_ref, kmax_ref, s0_ref, s1_ref, *, tq, tk, rsub, lambda_init):
    h = pl.program_id(1)
    qi = pl.program_id(2)
    slope = slope_ref[h] * LOG2_E
    q0 = qi * tq
    n_sub = 2 * tq // rsub

    @pl.when(qi == 0)
    def _():
        kf = k_ref[0].astype(F32)
        norms = _dot((kf * kf).astype(BF16), sel_ref[...])
        kmax_ref[...] = jnp.sqrt(jnp.max(norms, axis=0, keepdims=True)) * 1.02

    q = q_ref[0]
    lane = lax.broadcasted_iota(jnp.int32, (tq, HEAD_W), 1)
    qs_ref[0:tq, :] = jnp.where(lane < DIFF_QK_DIM, q, jnp.zeros_like(q))
    qs_ref[tq:, :] = jnp.where(lane >= DIFF_QK_DIM, q, jnp.zeros_like(q))
    m_ref[...] = jnp.full_like(m_ref, MASK_VALUE)
    l_ref[...] = jnp.zeros_like(l_ref)
    acc_ref[...] = jnp.zeros_like(acc_ref)

    def tile_start(j):
        return j * tk if isinstance(j, int) else pl.multiple_of(j * tk, tk)

    s_refs = (s0_ref, s1_ref)

    def issue_scores(j, slot):
        s_refs[slot][...] = _dot_nt(qs_ref[...], k_ref[0, pl.ds(tile_start(j), tk), :])

    def consume(j, slot, mode, prefetch):
        s_ref = s_refs[slot]
        if prefetch is not None:
            issue_scores(prefetch, 1 - slot)
        start = tile_start(j)
        kpos = j * tk + lax.broadcasted_iota(jnp.int32, (1, tk), 1)
        bias = slope * (kpos - q0).astype(F32)
        v = v_ref[0, pl.ds(start, tk), :]

        def stage_b(r):
            rows = slice(r * rsub, (r + 1) * rsub)
            s = s_ref[rows, :] + bias
            if mode == "first":
                s = jnp.where(kpos >= N_PAD, s, MASK_VALUE)
            elif mode == "diag":
                qpos = q0 + (r * rsub) % tq + lax.broadcasted_iota(jnp.int32, (rsub, 1), 0)
                s = jnp.where((qpos >= kpos) & (kpos >= N_PAD), s, MASK_VALUE)
            m_prev = m_ref[rows, :]
            m_new = jnp.maximum(m_prev, jnp.max(s, axis=1, keepdims=True))
            p = jnp.exp2(s - m_new)
            alpha = jnp.exp2(m_prev - m_new)
            l_ref[rows, :] = alpha * l_ref[rows, :] + jnp.sum(p, axis=1, keepdims=True)
            m_ref[rows, :] = m_new
            return p.astype(BF16), alpha

        def stage_c(r, p, alpha):
            rows = slice(r * rsub, (r + 1) * rsub)
            acc_ref[rows, :] = alpha * acc_ref[rows, :] + _dot(p, v)

        b_out = {}
        for t in range(n_sub + 1):
            if t < n_sub:
                b_out[t] = stage_b(t)
            if t >= 1:
                stage_c(t - 1, *b_out.pop(t - 1))

    issue_scores(qi, 0)
    consume(qi, 0, "diag", jnp.maximum(qi - 1, 0))

    qf = qs_ref[...].astype(F32)
    qnorm = jnp.sqrt(jnp.sum(qf * qf, axis=1, keepdims=True))
    row = lax.broadcasted_iota(jnp.int32, (2 * tq, 1), 0)
    kmax = jnp.where(row < tq, kmax_ref[:, 0:1], kmax_ref[:, 1:2])
    slack = jnp.max(qnorm * kmax - m_ref[...])

    def needed(rel):
        return slack + slope * rel >= -ZERO_PROB_LOG2

    def cond(carry):
        j, rel, _ = carry
        return (j >= 1) & needed(rel)

    def body(carry):
        j, rel, slot = carry
        for static_slot in (0, 1):
            @pl.when(slot == static_slot)
            def _():
                consume(j, static_slot, "plain", j - 1)
        return j - 1, rel - tk, 1 - slot

    j_end, rel_end, slot_end = lax.while_loop(cond, body, (qi - 1, jnp.float32(-1.0), jnp.int32(1)))

    for static_slot in (0, 1):
        @pl.when((j_end == 0) & needed(rel_end) & (slot_end == static_slot))
        def _():
            consume(0, static_slot, "first", None)

    o = acc_ref[...] / l_ref[...]
    lam = (jnp.exp(jnp.sum(lq1_ref[...] * lk1_ref[...], axis=1, keepdims=True))
           - jnp.exp(jnp.sum(lq2_ref[...] * lk2_ref[...], axis=1, keepdims=True)) + lambda_init)
    o = o[0:tq, :] - lam * o[tq:, :]
    o_ref[0] = (_rms(o, gn_ref[...]) * (1.0 - lambda_init)).astype(BF16)


def _diff_attention(proj, lam_q1, lam_k1, lam_q2, lam_k2, gn, lambda_init, col0):
    b, p, _ = proj.shape
    nh = N_DIFF_HEADS
    tq = tk = _pick_tile(p, (640, 512, 256, 128))
    rsub = 320 if tq % 320 == 0 else tq
    slopes = jnp.asarray(2.0 ** (-8.0 * (np.arange(nh) + 1.0) / nh), F32)
    lam_spec = _const_spec((1, DIFF_QK_DIM))
    kv_block = (1, p, HEAD_W)
    sel = np.zeros((HEAD_W, HEAD_W), np.float32)
    sel[:DIFF_QK_DIM, 0] = 1.0
    sel[DIFF_QK_DIM:, 1] = 1.0
    return pl.pallas_call(
        functools.partial(_diff_attn_kernel, tq=tq, tk=tk, rsub=rsub, lambda_init=lambda_init),
        grid=(b, nh, p // tq),
        in_specs=[pl.BlockSpec(memory_space=pltpu.SMEM),
                  lam_spec, lam_spec, lam_spec, lam_spec,
                  pl.BlockSpec((1, tq, HEAD_W), lambda bi, h, i: (bi, i, col0 + h)),
                  pl.BlockSpec(kv_block, lambda bi, h, i: (bi, 0, col0 + nh + h)),
                  pl.BlockSpec(kv_block, lambda bi, h, i: (bi, 0, col0 + 2 * nh + h)),
                  _const_spec((HEAD_W, HEAD_W)),
                  pl.BlockSpec((1, HEAD_W), lambda bi, h, i: (0, h))],
        out_specs=pl.BlockSpec((1, tq, HEAD_W), lambda bi, h, i: (bi, i, h)),
        out_shape=jax.ShapeDtypeStruct((b, p, nh * HEAD_W), BF16),
        scratch_shapes=[pltpu.VMEM((2 * tq, HEAD_W), BF16),
                        pltpu.VMEM((2 * tq, 1), F32),
                        pltpu.VMEM((2 * tq, 1), F32),
                        pltpu.VMEM((2 * tq, HEAD_W), F32),
                        pltpu.VMEM((1, HEAD_W), F32),
                        pltpu.VMEM((2 * tq, tk), F32),
                        pltpu.VMEM((2 * tq, tk), F32)],
        compiler_params=_params(("parallel", "parallel", "arbitrary")),
    )(slopes, lam_q1.reshape(1, -1), lam_k1.reshape(1, -1), lam_q2.reshape(1, -1), lam_k2.reshape(1, -1),
      proj, proj, proj, jnp.asarray(sel, BF16), gn.reshape(1, -1))


def _cumsum_blocks(tk):
    blocks, start = [], 0
    while start < tk:
        width = 256 if tk - start >= 256 else tk - start
        blocks.append((start, width))
        start += width
    return tuple(blocks)


def _sb_attn_kernel(q_ref, k_ref, v_ref, u_ref, o_ref, run_ref, acc_ref, *, tq):
    qi = pl.program_id(2)
    q0 = pl.multiple_of(qi * tq, BLOCK)

    def attend(pieces):
        n = len(pieces)
        z, softplus, log_beta, mask, later, w = ([None] * n for _ in range(6))
        for i, (row_lo, n_rows, kstart, width, masked, first) in enumerate(pieces):
            z[i] = _dot_nt(q_ref[0, row_lo:row_lo + n_rows, :], k_ref[0, pl.ds(kstart, width), :])
        for i, (row_lo, n_rows, kstart, width, masked, first) in enumerate(pieces):
            neg_abs = lax.bitcast_convert_type(
                lax.bitcast_convert_type(z[i], jnp.uint32) | jnp.uint32(0x80000000), F32)
            sp = jnp.maximum(z[i], 0.0) + jnp.log(1.0 + jnp.exp2(neg_abs)) * INV_LN2
            log_beta[i] = z[i] - sp
            if masked:
                qpos = q0 + row_lo + lax.broadcasted_iota(jnp.int32, (n_rows, 1), 0)
                kpos = kstart + lax.broadcasted_iota(jnp.int32, (1, width), 1)
                mask[i] = (qpos > kpos) & (kpos >= N_PAD)
                sp = jnp.where(mask[i], sp, 0.0)
            softplus[i] = sp
        for i, (row_lo, n_rows, kstart, width, masked, first) in enumerate(pieces):
            rows = slice(row_lo, row_lo + n_rows)
            run = jnp.zeros((n_rows, 1), F32) if first else run_ref[rows, :]
            sp16 = softplus[i].astype(BF16)
            blocks = _cumsum_blocks(width)
            parts = [None] * len(blocks)
            for idx in reversed(range(len(blocks))):
                b0, bw = blocks[idx]
                parts[idx] = _dot(sp16[:, b0:b0 + bw], u_ref[0:bw, 0:bw]) + run
                run = run - jnp.sum(softplus[i][:, b0:b0 + bw], axis=1, keepdims=True)
            run_ref[rows, :] = run
            later[i] = parts[0] if len(parts) == 1 else jnp.concatenate(parts, axis=1)
        for i, (row_lo, n_rows, kstart, width, masked, first) in enumerate(pieces):
            w[i] = jnp.exp2(log_beta[i] + later[i])
            if masked:
                w[i] = jnp.where(mask[i], w[i], 0.0)
            w[i] = w[i].astype(BF16)
        for i, (row_lo, n_rows, kstart, width, masked, first) in enumerate(pieces):
            rows = slice(row_lo, row_lo + n_rows)
            contrib = _dot(w[i], v_ref[0, pl.ds(kstart, width), :])
            if first:
                acc_ref[rows, :] = contrib
            else:
                acc_ref[rows, :] += contrib

    def before_tile(row_lo, n_rows, kstart, width, masked):
        return [(row_lo + r, BLOCK, kstart, width, masked, False) for r in range(0, n_rows, BLOCK)]

    top = min(SB_TOP_ROWS, tq)
    own = [(r * BLOCK, BLOCK, q0, (r + 1) * BLOCK, True, True) for r in range(tq // BLOCK)]

    @pl.when(qi == 0)
    def _():
        attend(own)

    @pl.when(qi > 0)
    def _():
        attend(own[:top // BLOCK])
        attend(own[top // BLOCK:]
               + before_tile(0, top, pl.multiple_of(q0 - SB_KEY_CHUNK, BLOCK), SB_KEY_CHUNK, False))

    def walk(row_lo, n_rows, chunks_done):
        def more_needed():
            return jnp.max(run_ref[row_lo:row_lo + n_rows, :]) > -ZERO_PROB_LOG2

        def chunk(kstart, width, masked):
            attend(before_tile(row_lo, n_rows, pl.multiple_of(kstart, BLOCK), width, masked))

        def cond(carry):
            kstart, go = carry
            return (kstart >= BLOCK) & go

        def body(carry):
            kstart, _ = carry
            chunk(kstart, SB_KEY_CHUNK, False)
            return kstart - SB_KEY_CHUNK, more_needed()

        kstart, go = lax.while_loop(cond, body, (q0 - (chunks_done + 1) * SB_KEY_CHUNK, more_needed()))

        def cond_tail(carry):
            kstart, go = carry
            return (kstart >= 0) & go

        def body_tail(carry):
            kstart, _ = carry
            chunk(kstart, BLOCK, True)
            return kstart - BLOCK, more_needed()

        lax.while_loop(cond_tail, body_tail, (kstart + SB_KEY_CHUNK - BLOCK, go))

    walk(0, top, 1)
    if top < tq:
        @pl.when(jnp.max(run_ref[top:, :]) > -ZERO_PROB_LOG2)
        def _():
            walk(top, tq - top, 0)

    o_ref[0] = acc_ref[...].astype(BF16)


def _sb_attention(qkv):
    b, p, _ = qkv.shape
    nh = N_SB_HEADS
    tq = _pick_tile(p, (640, 512, 256, 128))
    umax = SB_KEY_CHUNK
    jj = np.arange(umax)
    u = jnp.asarray(-(jj[:, None] > jj[None, :]).astype(np.float32), BF16)
    kv_block = (1, p, HEAD_W)
    return pl.pallas_call(
        functools.partial(_sb_attn_kernel, tq=tq),
        grid=(b, nh, p // tq),
        in_specs=[pl.BlockSpec((1, tq, HEAD_W), lambda bi, h, i: (bi, i, h)),
                  pl.BlockSpec(kv_block, lambda bi, h, i: (bi, 0, nh + h)),
                  pl.BlockSpec(kv_block, lambda bi, h, i: (bi, 0, 2 * nh + h)),
                  _const_spec((umax, umax))],
        out_specs=pl.BlockSpec((1, tq, HEAD_W), lambda bi, h, i: (bi, i, h)),
        out_shape=jax.ShapeDtypeStruct((b, p, nh * HEAD_W), BF16),
        scratch_shapes=[pltpu.VMEM((tq, 1), F32), pltpu.VMEM((tq, HEAD_W), F32)],
        compiler_params=_params(("parallel", "parallel", "arbitrary")),
    )(qkv, qkv, qkv, u)


def kernel(x, meta_tokens, mix_norm, ffn_norm, ffn_up, ffn_conv, ffn_conv_b, ffn_down, ab_w_in, ab_ret_norm,
           ab_diff_norm, ab_lam_q1, ab_lam_k1, ab_lam_q2, ab_lam_k2, ab_w_out, c_w_in, c_w_out, final_norm):
    b, _, d = x.shape
    depth = mix_norm.shape[0]
    pad = jnp.zeros((b, N_PAD, d), x.dtype)
    meta = jnp.broadcast_to(meta_tokens[None].astype(x.dtype), (b, N_META, d))
    h = jnp.concatenate([pad, meta, x], axis=1)
    p = h.shape[1]
    t = b * p
    ret_w = N_RET_HEADS * HEAD_W
    for i in range(depth):
        if i % 2 == 0:
            e = i // 2
            lambda_init = 0.8 - 0.6 * math.exp(-0.3 * i)
            w_in = ab_w_in[e].astype(BF16)
            scales = (1.0, 1.0, 1.0, 1.0, DIFF_QK_DIM ** -0.5 * LOG2_E, 1.0, 1.0)
            proj = _norm_proj(h.reshape(t, d), mix_norm[i], w_in, scales, 512)
            proj = proj.reshape(b, p, -1)
            ret = _retention(proj, ab_ret_norm[e])
            dif = _diff_attention(proj, ab_lam_q1[e], ab_lam_k1[e], ab_lam_q2[e], ab_lam_k2[e],
                                  ab_diff_norm[e], lambda_init, 4 * N_RET_HEADS)
            w_out = ab_w_out[e].astype(BF16)
            h = _res_proj(h.reshape(t, d), [ret.reshape(t, -1), dif.reshape(t, -1)],
                          [w_out[:ret_w], w_out[ret_w:]]).reshape(b, p, d)
        else:
            o = i // 2
            w_in = c_w_in[o].astype(BF16)
            n_chunks = w_in.shape[1] // 512
            q_chunks = N_SB_HEADS * HEAD_W // 512
            scales = tuple(HEAD_W ** -0.5 * LOG2_E if c < q_chunks else 1.0 for c in range(n_chunks))
            qkv = _norm_proj(h.reshape(t, d), mix_norm[i], w_in, scales, 512).reshape(b, p, -1)
            att = _sb_attention(qkv)
            h = _res_proj(h.reshape(t, d), [att.reshape(t, -1)], [c_w_out[o].astype(BF16)]).reshape(b, p, d)
        h = _ffn(h, ffn_norm[i], ffn_up[i].astype(BF16), ffn_conv[i], ffn_conv_b[i], ffn_down[i].astype(BF16),
                 final_norm if i == depth - 1 else None)
    return h[:, BLOCK:, :]
```

```python
import functools
import math

import numpy as np
import jax
import jax.numpy as jnp
from jax import lax
from jax.experimental import pallas as pl
from jax.experimental.pallas import tpu as pltpu

F32 = jnp.float32
BF16 = jnp.bfloat16

N_META = 16
BLOCK = 128
N_PAD = BLOCK - N_META
EPS = 1e-6
MASK_VALUE = -1e30
HEAD_W = 128
N_RET_HEADS = 4
N_DIFF_HEADS = 4
DIFF_QK_DIM = 64
N_SB_HEADS = 8
CONV_WIDTH = 3
HALO = 16
LOG2_E = math.log2(math.e)
INV_LN2 = 1.0 / math.log(2.0)
ZERO_PROB_LOG2 = 160.0
SB_KEY_CHUNK = 256
SB_TOP_ROWS = 256

VMEM_LIMIT_BYTES = 56 * 1024 * 1024


def _pick_tile(n, candidates):
    for c in candidates:
        if n % c == 0:
            return c
    raise ValueError(f"no tile for {n} in {candidates}")


def _params(sem):
    return pltpu.CompilerParams(dimension_semantics=sem, vmem_limit_bytes=VMEM_LIMIT_BYTES)


def _const_spec(shape):
    return pl.BlockSpec(shape, lambda *_: (0,) * len(shape), pipeline_mode=pl.Buffered(1))


def _rms(x, g):
    return x * lax.rsqrt(jnp.mean(x * x, axis=-1, keepdims=True) + EPS) * g


def _dot(a, b):
    return jnp.dot(a, b, preferred_element_type=F32)


def _dot_nt(a, b):
    return lax.dot_general(a, b, (((1,), (1,)), ((), ())), preferred_element_type=F32)


def _dot_tn(a, b):
    return lax.dot_general(a, b, (((0,), (0,)), ((), ())), preferred_element_type=F32)


def _norm_proj_kernel(x_ref, g_ref, w_ref, o_ref, xn_ref, *, chunk, scales):
    xn_ref[...] = _rms(x_ref[...], g_ref[...]).astype(BF16)
    for c, scale in enumerate(scales):
        cols = slice(c * chunk, (c + 1) * chunk)
        r = _dot(xn_ref[...], w_ref[:, cols])
        if scale != 1.0:
            r = r * scale
        o_ref[:, cols] = r.astype(BF16)


def _norm_proj(x2, g, w, scales, chunk):
    t, d = x2.shape
    n = w.shape[1]
    tm = _pick_tile(t, (1024, 640, 512, 256, 128))
    return pl.pallas_call(
        functools.partial(_norm_proj_kernel, chunk=chunk, scales=scales),
        grid=(t // tm,),
        in_specs=[pl.BlockSpec((tm, d), lambda i: (i, 0)),
                  _const_spec((1, d)),
                  _const_spec((d, n))],
        out_specs=pl.BlockSpec((tm, n), lambda i: (i, 0)),
        out_shape=jax.ShapeDtypeStruct((t, n), BF16),
        scratch_shapes=[pltpu.VMEM((tm, d), BF16)],
        compiler_params=_params(("parallel",)),
    )(x2, g.reshape(1, d), w)


def _embed_norm_proj_kernel(x_ref, lead_ref, g_ref, w_ref, o_ref, h_ref, xn_ref, *, chunk, scales):
    tm = h_ref.shape[1]

    @pl.when(pl.program_id(1) == 0)
    def _():
        h_ref[0, 0:BLOCK, :] = lead_ref[...]
        h_ref[0, BLOCK:, :] = x_ref[0, 0:tm - BLOCK, :]

    @pl.when(pl.program_id(1) > 0)
    def _():
        h_ref[0] = x_ref[0]

    xn_ref[...] = _rms(h_ref[0], g_ref[...]).astype(BF16)
    for c, scale in enumerate(scales):
        cols = slice(c * chunk, (c + 1) * chunk)
        r = _dot(xn_ref[...], w_ref[:, cols])
        if scale != 1.0:
            r = r * scale
        o_ref[0, :, cols] = r.astype(BF16)


def _embed_norm_proj(x, meta_tokens, g, w, scales, chunk):
    b, s, d = x.shape
    p = s + BLOCK
    n = w.shape[1]
    tm = _pick_tile(p, (640, 512, 256, 128))
    lead = jnp.concatenate([jnp.zeros((N_PAD, d), x.dtype), meta_tokens.astype(x.dtype)], axis=0)
    x_spec = pl.BlockSpec((pl.Element(1), pl.Element(tm), pl.Element(d)),
                          lambda bi, i: (bi, pl.multiple_of(jnp.maximum(i * tm - BLOCK, 0), BLOCK), 0))
    return pl.pallas_call(
        functools.partial(_embed_norm_proj_kernel, chunk=chunk, scales=scales),
        grid=(b, p // tm),
        in_specs=[x_spec, _const_spec((BLOCK, d)), _const_spec((1, d)), _const_spec((d, n))],
        out_specs=[pl.BlockSpec((1, tm, n), lambda bi, i: (bi, i, 0)),
                   pl.BlockSpec((1, tm, d), lambda bi, i: (bi, i, 0))],
        out_shape=[jax.ShapeDtypeStruct((b, p, n), BF16), jax.ShapeDtypeStruct((b, p, d), F32)],
        scratch_shapes=[pltpu.VMEM((tm, d), BF16)],
        compiler_params=_params(("parallel", "parallel")),
    )(x, lead, g.reshape(1, d), w)


def _res_proj_kernel(*refs, n_in):
    h_ref, o_ref = refs[0], refs[-1]
    acc = h_ref[...]
    for a_ref, w_ref in zip(refs[1:1 + n_in], refs[1 + n_in:1 + 2 * n_in]):
        acc = acc + _dot(a_ref[...], w_ref[...])
    o_ref[...] = acc


def _res_proj(h2, acts, weights):
    t, d = h2.shape
    tm = _pick_tile(t, (1024, 640, 512, 256, 128))
    in_specs = [pl.BlockSpec((tm, d), lambda i: (i, 0))]
    in_specs += [pl.BlockSpec((tm, a.shape[1]), lambda i: (i, 0)) for a in acts]
    in_specs += [_const_spec(w.shape) for w in weights]
    return pl.pallas_call(
        functools.partial(_res_proj_kernel, n_in=len(acts)),
        grid=(t // tm,),
        in_specs=in_specs,
        out_specs=pl.BlockSpec((tm, d), lambda i: (i, 0)),
        out_shape=jax.ShapeDtypeStruct((t, d), F32),
        compiler_params=_params(("parallel",)),
    )(h2, *acts, *weights)


def _ffn_kernel(*refs, tm, f, fc, final, p_start):
    if final:
        x_ref, xh_ref, g_ref, wup_ref, wc_ref, bc_ref, wd_ref, fg_ref, o_ref, xn_ref, acc_ref = refs
    else:
        x_ref, xh_ref, g_ref, wup_ref, wc_ref, bc_ref, wd_ref, o_ref, xn_ref, acc_ref = refs
    i = pl.program_id(1)
    g = g_ref[...]
    xn_ref[0:HALO, :] = _rms(xh_ref[0], g).astype(BF16)
    xn_ref[HALO:, :] = _rms(x_ref[0], g).astype(BF16)
    if p_start < N_PAD + HALO:
        masked_rows = HALO + N_PAD - p_start

        @pl.when(i == 0)
        def _():
            xn_ref[0:masked_rows, :] = jnp.zeros((masked_rows, xn_ref.shape[1]), BF16)

    def up_proj(c):
        gate = _dot(xn_ref[...], wup_ref[:, c * fc:(c + 1) * fc])
        val = _dot(xn_ref[HALO:, :], wup_ref[:, f + c * fc:f + (c + 1) * fc])
        return gate, val

    n_chunks = f // fc
    nxt = up_proj(0)
    for c in range(n_chunks):
        cols = slice(c * fc, (c + 1) * fc)
        gate, val = nxt
        if c + 1 < n_chunks:
            nxt = up_proj(c + 1)
        conv = bc_ref[:, cols]
        for tap in range(CONV_WIDTH):
            lo = HALO - (CONV_WIDTH - 1) + tap
            conv = conv + gate[lo:lo + tm, :] * wc_ref[tap:tap + 1, cols]
        half = 0.5 * conv
        mid = ((half + half * jnp.tanh(half)) * val).astype(BF16)
        contrib = _dot(mid, wd_ref[cols, :])
        if c == 0:
            acc_ref[...] = contrib
        else:
            acc_ref[...] += contrib
    y = x_ref[0] + acc_ref[...]
    if final:
        y = _rms(y, fg_ref[...])
    o_ref[0] = y


def _ffn(h, g, w_up, w_conv, b_conv, w_down, final_g=None):
    b, p, d = h.shape
    f = w_down.shape[0]
    fc = 256
    final = final_g is not None
    p_start = BLOCK if final else 0
    tm = _pick_tile(p - p_start, (1024, 832, 640, 512, 256, 128) if final else (832, 640, 512, 256, 128))

    def rows_spec(n_rows, back):
        return pl.BlockSpec(
            (pl.Element(1), pl.Element(n_rows), pl.Element(d)),
            lambda bi, i: (bi, pl.multiple_of(jnp.maximum(p_start + i * tm - back, 0), HALO), 0))

    in_specs = [rows_spec(tm, 0),
                rows_spec(HALO, HALO),
                _const_spec((1, d)),
                _const_spec((d, 2 * f)),
                _const_spec((CONV_WIDTH, f)),
                _const_spec((1, f)),
                _const_spec((f, d))]
    args = [h, h, g.reshape(1, d), w_up, w_conv, b_conv.reshape(1, f), w_down]
    if final:
        in_specs.append(_const_spec((1, d)))
        args.append(final_g.reshape(1, d))
    return pl.pallas_call(
        functools.partial(_ffn_kernel, tm=tm, f=f, fc=fc, final=final, p_start=p_start),
        grid=(b, (p - p_start) // tm),
        in_specs=in_specs,
        out_specs=pl.BlockSpec((1, tm, d), lambda bi, i: (bi, i, 0)),
        out_shape=jax.ShapeDtypeStruct((b, p - p_start, d), F32),
        scratch_shapes=[pltpu.VMEM((tm + HALO, d), BF16), pltpu.VMEM((tm, d), F32)],
        compiler_params=_params(("parallel", "parallel")),
    )(*args)


def _retention_constants():
    c = BLOCK
    hh = np.arange(N_RET_HEADS, dtype=np.float64)
    log_g = np.log1p(-(2.0 ** (-5.0 - hh)))
    j = np.arange(c, dtype=np.float64)
    rel = j[:, None] - j[None, :]
    scale = HEAD_W ** -0.5
    decay = np.where(rel >= 0, np.exp(log_g[:, None, None] * np.maximum(rel, 0.0)), 0.0) * scale
    q_decay = np.broadcast_to(np.exp(log_g[:, None] * (j + 1.0))[:, :, None], (N_RET_HEADS, c, HEAD_W))
    k_decay = np.broadcast_to((np.exp(log_g[:, None] * (c - 1.0 - j)) * scale)[:, :, None],
                              (N_RET_HEADS, c, HEAD_W))
    chunk_decay = tuple(float(v) for v in np.exp(log_g * c))
    return (jnp.asarray(decay, F32), jnp.asarray(q_decay, F32), jnp.asarray(k_decay, F32), chunk_decay)


def _retention_kernel(q_ref, k_ref, v_ref, gate_ref, dmat_ref, qd_ref, kd_ref, gn_ref, o_ref, state_ref,
                      *, tr, chunk_decay):
    i = pl.program_id(1)

    @pl.when(i == 0)
    def _():
        state_ref[...] = jnp.zeros_like(state_ref)

    for c in range(tr // BLOCK):
        rows = slice(c * BLOCK, (c + 1) * BLOCK)
        if c == 0:
            pos = i * tr + lax.broadcasted_iota(jnp.int32, (BLOCK, HEAD_W), 0)
            valid = pos >= N_PAD
        for h in range(N_RET_HEADS):
            cols = slice(h * HEAD_W, (h + 1) * HEAD_W)
            q = q_ref[0, rows, cols]
            k = k_ref[0, rows, cols]
            v = v_ref[0, rows, cols]
            if c == 0:
                k = jnp.where(valid, k, jnp.zeros_like(k))
                v = jnp.where(valid, v, jnp.zeros_like(v))
            scores = _dot_nt(q, k) * dmat_ref[h]
            inner = _dot(scores.astype(BF16), v)
            state = state_ref[h]
            cross = _dot(q, state.astype(BF16)) * qd_ref[h]
            k_scaled = (k.astype(F32) * kd_ref[h]).astype(BF16)
            state_ref[h] = chunk_decay[h] * state + _dot_tn(k_scaled, v)
            o = inner + cross
            oc = o - jnp.mean(o, axis=-1, keepdims=True)
            y = oc * lax.rsqrt(jnp.mean(oc * oc, axis=-1, keepdims=True) + EPS) * gn_ref[:, cols]
            gate = gate_ref[0, rows, cols].astype(F32)
            o_ref[0, rows, cols] = (y * (gate / (1.0 + jnp.exp(-gate)))).astype(BF16)


def _retention(proj, gn):
    b, p, _ = proj.shape
    w = N_RET_HEADS * HEAD_W
    tr = _pick_tile(p, (640, 512, 256, 128))
    dmat, qd, kd, chunk_decay = _retention_constants()

    def col_spec(cb):
        return pl.BlockSpec((1, tr, w), lambda bi, i: (bi, i, cb))

    cshape = (N_RET_HEADS, BLOCK, HEAD_W)
    return pl.pallas_call(
        functools.partial(_retention_kernel, tr=tr, chunk_decay=chunk_decay),
        grid=(b, p // tr),
        in_specs=[col_spec(0), col_spec(1), col_spec(2), col_spec(3),
                  _const_spec(cshape), _const_spec(cshape), _const_spec(cshape), _const_spec((1, w))],
        out_specs=pl.BlockSpec((1, tr, w), lambda bi, i: (bi, i, 0)),
        out_shape=jax.ShapeDtypeStruct((b, p, w), BF16),
        scratch_shapes=[pltpu.VMEM(cshape, F32)],
        compiler_params=_params(("parallel", "arbitrary")),
    )(proj, proj, proj, proj, dmat, qd, kd, gn.reshape(1, w))


def _diff_attn_kernel(slope_ref, lq1_ref, lk1_ref, lq2_ref, lk2_ref, q_ref, k_ref, v_ref, sel_ref, gn_ref, o_ref,
                      qs_ref, m_ref, l_ref, acc_ref, kmax_ref, s0_ref, s1_ref, *, tq, tk, rsub, lambda_init):
    h = pl.program_id(1)
    qi = pl.program_id(2)
    slope = slope_ref[h] * LOG2_E
    q0 = qi * tq
    n_sub = 2 * tq // rsub

    @pl.when(qi == 0)
    def _():
        kf = k_ref[0].astype(F32)
        norms = _dot((kf * kf).astype(BF16), sel_ref[...])
        kmax_ref[...] = jnp.sqrt(jnp.max(norms, axis=0, keepdims=True)) * 1.02

    q = q_ref[0]
    lane = lax.broadcasted_iota(jnp.int32, (tq, HEAD_W), 1)
    qs_ref[0:tq, :] = jnp.where(lane < DIFF_QK_DIM, q, jnp.zeros_like(q))
    qs_ref[tq:, :] = jnp.where(lane >= DIFF_QK_DIM, q, jnp.zeros_like(q))
    m_ref[...] = jnp.full_like(m_ref, MASK_VALUE)
    l_ref[...] = jnp.zeros_like(l_ref)
    acc_ref[...] = jnp.zeros_like(acc_ref)

    def tile_start(j):
        return j * tk if isinstance(j, int) else pl.multiple_of(j * tk, tk)

    s_refs = (s0_ref, s1_ref)

    def issue_scores(j, slot):
        s_refs[slot][...] = _dot_nt(qs_ref[...], k_ref[0, pl.ds(tile_start(j), tk), :])

    def consume(j, slot, mode, prefetch):
        s_ref = s_refs[slot]
        if prefetch is not None:
            issue_scores(prefetch, 1 - slot)
        start = tile_start(j)
        kpos = j * tk + lax.broadcasted_iota(jnp.int32, (1, tk), 1)
        bias = slope * (kpos - q0).astype(F32)
        v = v_ref[0, pl.ds(start, tk), :]

        def stage_b(r):
            rows = slice(r * rsub, (r + 1) * rsub)
            s = s_ref[rows, :] + bias
            if mode == "first":
                s = jnp.where(kpos >= N_PAD, s, MASK_VALUE)
            elif mode == "diag":
                qpos = q0 + (r * rsub) % tq + lax.broadcasted_iota(jnp.int32, (rsub, 1), 0)
                s = jnp.where((qpos >= kpos) & (kpos >= N_PAD), s, MASK_VALUE)
            m_prev = m_ref[rows, :]
            m_new = jnp.maximum(m_prev, jnp.max(s, axis=1, keepdims=True))
            p = jnp.exp2(s - m_new)
            alpha = jnp.exp2(m_prev - m_new)
            l_ref[rows, :] = alpha * l_ref[rows, :] + jnp.sum(p, axis=1, keepdims=True)
            m_ref[rows, :] = m_new
            return p.astype(BF16), alpha

        def stage_c(r, p, alpha):
            rows = slice(r * rsub, (r + 1) * rsub)
            acc_ref[rows, :] = alpha * acc_ref[rows, :] + _dot(p, v)

        b_out = {}
        for t in range(n_sub + 1):
            if t < n_sub:
                b_out[t] = stage_b(t)
            if t >= 1:
                stage_c(t - 1, *b_out.pop(t - 1))

    issue_scores(qi, 0)
    consume(qi, 0, "diag", jnp.maximum(qi - 1, 0))

    qf = qs_ref[...].astype(F32)
    qnorm = jnp.sqrt(jnp.sum(qf * qf, axis=1, keepdims=True))
    row = lax.broadcasted_iota(jnp.int32, (2 * tq, 1), 0)
    kmax = jnp.where(row < tq, kmax_ref[:, 0:1], kmax_ref[:, 1:2])
    slack = jnp.max(qnorm * kmax - m_ref[...])

    def needed(rel):
        return slack + slope * rel >= -ZERO_PROB_LOG2

    def cond(carry):
        j, rel, _ = carry
        return (j >= 1) & needed(rel)

    def body(carry):
        j, rel, slot = carry
        for static_slot in (0, 1):
            @pl.when(slot == static_slot)
            def _():
                consume(j, static_slot, "plain", j - 1)
        return j - 1, rel - tk, 1 - slot

    j_end, rel_end, slot_end = lax.while_loop(cond, body, (qi - 1, jnp.float32(-1.0), jnp.int32(1)))

    for static_slot in (0, 1):
        @pl.when((j_end == 0) & needed(rel_end) & (slot_end == static_slot))
        def _():
            consume(0, static_slot, "first", None)

    o = acc_ref[...] / l_ref[...]
    lam = (jnp.exp(jnp.sum(lq1_ref[...] * lk1_ref[...], axis=1, keepdims=True))
           - jnp.exp(jnp.sum(lq2_ref[...] * lk2_ref[...], axis=1, keepdims=True)) + lambda_init)
    o = o[0:tq, :] - lam * o[tq:, :]
    o_ref[0] = (_rms(o, gn_ref[...]) * (1.0 - lambda_init)).astype(BF16)


def _diff_attention(proj, lam_q1, lam_k1, lam_q2, lam_k2, gn, lambda_init, col0):
    b, p, _ = proj.shape
    nh = N_DIFF_HEADS
    tq = tk = _pick_tile(p, (640, 512, 256, 128))
    rsub = 320 if tq % 320 == 0 else tq
    slopes = jnp.asarray(2.0 ** (-8.0 * (np.arange(nh) + 1.0) / nh), F32)
    lam_spec = _const_spec((1, DIFF_QK_DIM))
    kv_block = (1, p, HEAD_W)
    sel = np.zeros((HEAD_W, HEAD_W), np.float32)
    sel[:DIFF_QK_DIM, 0] = 1.0
    sel[DIFF_QK_DIM:, 1] = 1.0
    return pl.pallas_call(
        functools.partial(_diff_attn_kernel, tq=tq, tk=tk, rsub=rsub, lambda_init=lambda_init),
        grid=(b, nh, p // tq),
        in_specs=[pl.BlockSpec(memory_space=pltpu.SMEM),
                  lam_spec, lam_spec, lam_spec, lam_spec,
                  pl.BlockSpec((1, tq, HEAD_W), lambda bi, h, i: (bi, i, col0 + h)),
                  pl.BlockSpec(kv_block, lambda bi, h, i: (bi, 0, col0 + nh + h)),
                  pl.BlockSpec(kv_block, lambda bi, h, i: (bi, 0, col0 + 2 * nh + h)),
                  _const_spec((HEAD_W, HEAD_W)),
                  pl.BlockSpec((1, HEAD_W), lambda bi, h, i: (0, h))],
        out_specs=pl.BlockSpec((1, tq, HEAD_W), lambda bi, h, i: (bi, i, h)),
        out_shape=jax.ShapeDtypeStruct((b, p, nh * HEAD_W), BF16),
        scratch_shapes=[pltpu.VMEM((2 * tq, HEAD_W), BF16),
                        pltpu.VMEM((2 * tq, 1), F32),
                        pltpu.VMEM((2 * tq, 1), F32),
                        pltpu.VMEM((2 * tq, HEAD_W), F32),
                        pltpu.VMEM((1, HEAD_W), F32),
                        pltpu.VMEM((2 * tq, tk), F32),
                        pltpu.VMEM((2 * tq, tk), F32)],
        compiler_params=_params(("parallel", "parallel", "arbitrary")),
    )(slopes, lam_q1.reshape(1, -1), lam_k1.reshape(1, -1), lam_q2.reshape(1, -1), lam_k2.reshape(1, -1),
      proj, proj, proj, jnp.asarray(sel, BF16), gn.reshape(1, -1))


def _cumsum_blocks(tk):
    blocks, start = [], 0
    while start < tk:
        width = 256 if tk - start >= 256 else tk - start
        blocks.append((start, width))
        start += width
    return tuple(blocks)


def _sb_attn_kernel(q_ref, k_ref, v_ref, u_ref, o_ref, run_ref, acc_ref, *, tq):
    qi = pl.program_id(2)
    q0 = pl.multiple_of(qi * tq, BLOCK)

    def attend(pieces):
        n = len(pieces)
        z, softplus, log_beta, mask, later, w = ([None] * n for _ in range(6))
        for i, (row_lo, n_rows, kstart, width, masked, first) in enumerate(pieces):
            z[i] = _dot_nt(q_ref[0, row_lo:row_lo + n_rows, :], k_ref[0, pl.ds(kstart, width), :])
        for i, (row_lo, n_rows, kstart, width, masked, first) in enumerate(pieces):
            neg_abs = lax.bitcast_convert_type(
                lax.bitcast_convert_type(z[i], jnp.uint32) | jnp.uint32(0x80000000), F32)
            sp = jnp.maximum(z[i], 0.0) + jnp.log(1.0 + jnp.exp2(neg_abs)) * INV_LN2
            log_beta[i] = z[i] - sp
            if masked:
                qpos = q0 + row_lo + lax.broadcasted_iota(jnp.int32, (n_rows, 1), 0)
                kpos = kstart + lax.broadcasted_iota(jnp.int32, (1, width), 1)
                mask[i] = (qpos > kpos) & (kpos >= N_PAD)
                sp = jnp.where(mask[i], sp, 0.0)
            softplus[i] = sp
        for i, (row_lo, n_rows, kstart, width, masked, first) in enumerate(pieces):
            rows = slice(row_lo, row_lo + n_rows)
            run = jnp.zeros((n_rows, 1), F32) if first else run_ref[rows, :]
            sp16 = softplus[i].astype(BF16)
            blocks = _cumsum_blocks(width)
            parts = [None] * len(blocks)
            for idx in reversed(range(len(blocks))):
                b0, bw = blocks[idx]
                parts[idx] = _dot(sp16[:, b0:b0 + bw], u_ref[0:bw, 0:bw]) + run
                run = run - jnp.sum(softplus[i][:, b0:b0 + bw], axis=1, keepdims=True)
            run_ref[rows, :] = run
            later[i] = parts[0] if len(parts) == 1 else jnp.concatenate(parts, axis=1)
        for i, (row_lo, n_rows, kstart, width, masked, first) in enumerate(pieces):
            w[i] = jnp.exp2(log_beta[i] + later[i])
            if masked:
                w[i] = jnp.where(mask[i], w[i], 0.0)
            w[i] = w[i].astype(BF16)
        for i, (row_lo, n_rows, kstart, width, masked, first) in enumerate(pieces):
            rows = slice(row_lo, row_lo + n_rows)
            contrib = _dot(w[i], v_ref[0, pl.ds(kstart, width), :])
            if first:
                acc_ref[rows, :] = contrib
            else:
                acc_ref[rows, :] += contrib

    def before_tile(row_lo, n_rows, kstart, width, masked):
        return [(row_lo + r, BLOCK, kstart, width, masked, False) for r in range(0, n_rows, BLOCK)]

    top = min(SB_TOP_ROWS, tq)
    own = [(r * BLOCK, BLOCK, q0, (r + 1) * BLOCK, True, True) for r in range(tq // BLOCK)]

    @pl.when(qi == 0)
    def _():
        attend(own)

    @pl.when(qi > 0)
    def _():
        attend(own[:top // BLOCK])
        attend(own[top // BLOCK:]
               + before_tile(0, top, pl.multiple_of(q0 - SB_KEY_CHUNK, BLOCK), SB_KEY_CHUNK, False))

    def walk(row_lo, n_rows, chunks_done):
        def more_needed():
            return jnp.max(run_ref[row_lo:row_lo + n_rows, :]) > -ZERO_PROB_LOG2

        def chunk(kstart, width, masked):
            attend(before_tile(row_lo, n_rows, pl.multiple_of(kstart, BLOCK), width, masked))

        def cond(carry):
            kstart, go = carry
            return (kstart >= BLOCK) & go

        def body(carry):
            kstart, _ = carry
            chunk(kstart, SB_KEY_CHUNK, False)
            return kstart - SB_KEY_CHUNK, more_needed()

        kstart, go = lax.while_loop(cond, body, (q0 - (chunks_done + 1) * SB_KEY_CHUNK, more_needed()))

        def cond_tail(carry):
            kstart, go = carry
            return (kstart >= 0) & go

        def body_tail(carry):
            kstart, _ = carry
            chunk(kstart, BLOCK, True)
            return kstart - BLOCK, more_needed()

        lax.while_loop(cond_tail, body_tail, (kstart + SB_KEY_CHUNK - BLOCK, go))

    walk(0, top, 1)
    if top < tq:
        @pl.when(jnp.max(run_ref[top:, :]) > -ZERO_PROB_LOG2)
        def _():
            walk(top, tq - top, 0)

    o_ref[0] = acc_ref[...].astype(BF16)


def _sb_attention(qkv):
    b, p, _ = qkv.shape
    nh = N_SB_HEADS
    tq = _pick_tile(p, (640, 512, 256, 128))
    umax = SB_KEY_CHUNK
    jj = np.arange(umax)
    u = jnp.asarray(-(jj[:, None] > jj[None, :]).astype(np.float32), BF16)
    kv_block = (1, p, HEAD_W)
    return pl.pallas_call(
        functools.partial(_sb_attn_kernel, tq=tq),
        grid=(b, nh, p // tq),
        in_specs=[pl.BlockSpec((1, tq, HEAD_W), lambda bi, h, i: (bi, i, h)),
                  pl.BlockSpec(kv_block, lambda bi, h, i: (bi, 0, nh + h)),
                  pl.BlockSpec(kv_block, lambda bi, h, i: (bi, 0, 2 * nh + h)),
                  _const_spec((umax, umax))],
        out_specs=pl.BlockSpec((1, tq, HEAD_W), lambda bi, h, i: (bi, i, h)),
        out_shape=jax.ShapeDtypeStruct((b, p, nh * HEAD_W), BF16),
        scratch_shapes=[pltpu.VMEM((tq, 1), F32), pltpu.VMEM((tq, HEAD_W), F32)],
        compiler_params=_params(("parallel", "parallel", "arbitrary")),
    )(qkv, qkv, qkv, u)


def kernel(x, meta_tokens, mix_norm, ffn_norm, ffn_up, ffn_conv, ffn_conv_b, ffn_down, ab_w_in, ab_ret_norm,
           ab_diff_norm, ab_lam_q1, ab_lam_k1, ab_lam_q2, ab_lam_k2, ab_w_out, c_w_in, c_w_out, final_norm):
    b, s, d = x.shape
    depth = mix_norm.shape[0]
    p = s + BLOCK
    t = b * p
    ret_w = N_RET_HEADS * HEAD_W
    h = None
    for i in range(depth):
        if i % 2 == 0:
            e = i // 2
            lambda_init = 0.8 - 0.6 * math.exp(-0.3 * i)
            w_in = ab_w_in[e].astype(BF16)
            scales = (1.0, 1.0, 1.0, 1.0, DIFF_QK_DIM ** -0.5 * LOG2_E, 1.0, 1.0)
            if i == 0:
                proj, h = _embed_norm_proj(x, meta_tokens, mix_norm[i], w_in, scales, 512)
            else:
                proj = _norm_proj(h.reshape(t, d), mix_norm[i], w_in, scales, 512).reshape(b, p, -1)
            ret = _retention(proj, ab_ret_norm[e])
            dif = _diff_attention(proj, ab_lam_q1[e], ab_lam_k1[e], ab_lam_q2[e], ab_lam_k2[e],
                                  ab_diff_norm[e], lambda_init, 4 * N_RET_HEADS)
            w_out = ab_w_out[e].astype(BF16)
            h = _res_proj(h.reshape(t, d), [ret.reshape(t, -1), dif.reshape(t, -1)],
                          [w_out[:ret_w], w_out[ret_w:]]).reshape(b, p, d)
        else:
            o = i // 2
            w_in = c_w_in[o].astype(BF16)
            n_chunks = w_in.shape[1] // 512
            q_chunks = N_SB_HEADS * HEAD_W // 512
            scales = tuple(HEAD_W ** -0.5 * LOG2_E if c < q_chunks else 1.0 for c in range(n_chunks))
            qkv = _norm_proj(h.reshape(t, d), mix_norm[i], w_in, scales, 512).reshape(b, p, -1)
            att = _sb_attention(qkv)
            h = _res_proj(h.reshape(t, d), [att.reshape(t, -1)], [c_w_out[o].astype(BF16)]).reshape(b, p, d)
        h = _ffn(h, ffn_norm[i], ffn_up[i].astype(BF16), ffn_conv[i], ffn_conv_b[i], ffn_down[i].astype(BF16),
                 final_norm if i == depth - 1 else None)
    return h
```

```python
import functools
import math

import numpy as np
import jax
import jax.numpy as jnp
from jax import lax
from jax.experimental import pallas as pl
from jax.experimental.pallas import tpu as pltpu

F32 = jnp.float32
BF16 = jnp.bfloat16

N_META = 16
BLOCK = 128
N_PAD = BLOCK - N_META
EPS = 1e-6
MASK_VALUE = -1e30
HEAD_W = 128
N_RET_HEADS = 4
N_DIFF_HEADS = 4
DIFF_QK_DIM = 64
N_SB_HEADS = 8
CONV_WIDTH = 3
HALO = 16
LOG2_E = math.log2(math.e)
INV_LN2 = 1.0 / math.log(2.0)
ZERO_PROB_LOG2 = 160.0
SB_KEY_CHUNK = 256
SB_TOP_ROWS = 256
SB_HEADS_PER_STEP = 2

VMEM_LIMIT_BYTES = 56 * 1024 * 1024


def _pick_tile(n, candidates):
    for c in candidates:
        if n % c == 0:
            return c
    raise ValueError(f"no tile for {n} in {candidates}")


def _params(sem):
    return pltpu.CompilerParams(dimension_semantics=sem, vmem_limit_bytes=VMEM_LIMIT_BYTES)


def _const_spec(shape):
    return pl.BlockSpec(shape, lambda *_: (0,) * len(shape), pipeline_mode=pl.Buffered(1))


def _rms(x, g):
    return x * lax.rsqrt(jnp.mean(x * x, axis=-1, keepdims=True) + EPS) * g


def _dot(a, b):
    return jnp.dot(a, b, preferred_element_type=F32)


def _dot_nt(a, b):
    return lax.dot_general(a, b, (((1,), (1,)), ((), ())), preferred_element_type=F32)


def _dot_tn(a, b):
    return lax.dot_general(a, b, (((0,), (0,)), ((), ())), preferred_element_type=F32)


def _norm_proj_kernel(x_ref, g_ref, w_ref, o_ref, xn_ref, *, chunk, scales):
    xn_ref[...] = _rms(x_ref[...], g_ref[...]).astype(BF16)
    for c, scale in enumerate(scales):
        cols = slice(c * chunk, (c + 1) * chunk)
        r = _dot(xn_ref[...], w_ref[:, cols])
        if scale != 1.0:
            r = r * scale
        o_ref[:, cols] = r.astype(BF16)


def _norm_proj(x2, g, w, scales, chunk):
    t, d = x2.shape
    n = w.shape[1]
    tm = _pick_tile(t, (1024, 640, 512, 256, 128))
    return pl.pallas_call(
        functools.partial(_norm_proj_kernel, chunk=chunk, scales=scales),
        grid=(t // tm,),
        in_specs=[pl.BlockSpec((tm, d), lambda i: (i, 0)),
                  _const_spec((1, d)),
                  _const_spec((d, n))],
        out_specs=pl.BlockSpec((tm, n), lambda i: (i, 0)),
        out_shape=jax.ShapeDtypeStruct((t, n), BF16),
        scratch_shapes=[pltpu.VMEM((tm, d), BF16)],
        compiler_params=_params(("parallel",)),
    )(x2, g.reshape(1, d), w)


def _embed_norm_proj_kernel(x_ref, lead_ref, g_ref, w_ref, o_ref, h_ref, xn_ref, *, chunk, scales):
    tm = h_ref.shape[1]

    @pl.when(pl.program_id(1) == 0)
    def _():
        h_ref[0, 0:BLOCK, :] = lead_ref[...]
        h_ref[0, BLOCK:, :] = x_ref[0, 0:tm - BLOCK, :]

    @pl.when(pl.program_id(1) > 0)
    def _():
        h_ref[0] = x_ref[0]

    xn_ref[...] = _rms(h_ref[0], g_ref[...]).astype(BF16)
    for c, scale in enumerate(scales):
        cols = slice(c * chunk, (c + 1) * chunk)
        r = _dot(xn_ref[...], w_ref[:, cols])
        if scale != 1.0:
            r = r * scale
        o_ref[0, :, cols] = r.astype(BF16)


def _embed_norm_proj(x, meta_tokens, g, w, scales, chunk):
    b, s, d = x.shape
    p = s + BLOCK
    n = w.shape[1]
    tm = _pick_tile(p, (640, 512, 256, 128))
    lead = jnp.concatenate([jnp.zeros((N_PAD, d), x.dtype), meta_tokens.astype(x.dtype)], axis=0)
    x_spec = pl.BlockSpec((pl.Element(1), pl.Element(tm), pl.Element(d)),
                          lambda bi, i: (bi, pl.multiple_of(jnp.maximum(i * tm - BLOCK, 0), BLOCK), 0))
    return pl.pallas_call(
        functools.partial(_embed_norm_proj_kernel, chunk=chunk, scales=scales),
        grid=(b, p // tm),
        in_specs=[x_spec, _const_spec((BLOCK, d)), _const_spec((1, d)), _const_spec((d, n))],
        out_specs=[pl.BlockSpec((1, tm, n), lambda bi, i: (bi, i, 0)),
                   pl.BlockSpec((1, tm, d), lambda bi, i: (bi, i, 0))],
        out_shape=[jax.ShapeDtypeStruct((b, p, n), BF16), jax.ShapeDtypeStruct((b, p, d), F32)],
        scratch_shapes=[pltpu.VMEM((tm, d), BF16)],
        compiler_params=_params(("parallel", "parallel")),
    )(x, lead, g.reshape(1, d), w)


def _res_proj_kernel(*refs, n_in):
    h_ref, o_ref = refs[0], refs[-1]
    acc = h_ref[...]
    for a_ref, w_ref in zip(refs[1:1 + n_in], refs[1 + n_in:1 + 2 * n_in]):
        acc = acc + _dot(a_ref[...], w_ref[...])
    o_ref[...] = acc


def _res_proj(h2, acts, weights):
    t, d = h2.shape
    tm = _pick_tile(t, (1024, 640, 512, 256, 128))
    in_specs = [pl.BlockSpec((tm, d), lambda i: (i, 0))]
    in_specs += [pl.BlockSpec((tm, a.shape[1]), lambda i: (i, 0)) for a in acts]
    in_specs += [_const_spec(w.shape) for w in weights]
    return pl.pallas_call(
        functools.partial(_res_proj_kernel, n_in=len(acts)),
        grid=(t // tm,),
        in_specs=in_specs,
        out_specs=pl.BlockSpec((tm, d), lambda i: (i, 0)),
        out_shape=jax.ShapeDtypeStruct((t, d), F32),
        compiler_params=_params(("parallel",)),
    )(h2, *acts, *weights)


def _ffn_kernel(*refs, tm, f, fc, final, p_start):
    if final:
        x_ref, xh_ref, g_ref, wup_ref, wc_ref, bc_ref, wd_ref, fg_ref, o_ref, xn_ref, acc_ref = refs
    else:
        x_ref, xh_ref, g_ref, wup_ref, wc_ref, bc_ref, wd_ref, o_ref, xn_ref, acc_ref = refs
    i = pl.program_id(1)
    g = g_ref[...]
    xn_ref[0:HALO, :] = _rms(xh_ref[0], g).astype(BF16)
    xn_ref[HALO:, :] = _rms(x_ref[0], g).astype(BF16)
    if p_start < N_PAD + HALO:
        masked_rows = HALO + N_PAD - p_start

        @pl.when(i == 0)
        def _():
            xn_ref[0:masked_rows, :] = jnp.zeros((masked_rows, xn_ref.shape[1]), BF16)

    def up_proj(c):
        gate = _dot(xn_ref[...], wup_ref[:, c * fc:(c + 1) * fc])
        val = _dot(xn_ref[HALO:, :], wup_ref[:, f + c * fc:f + (c + 1) * fc])
        return gate, val

    n_chunks = f // fc
    nxt = up_proj(0)
    for c in range(n_chunks):
        cols = slice(c * fc, (c + 1) * fc)
        gate, val = nxt
        if c + 1 < n_chunks:
            nxt = up_proj(c + 1)
        conv = bc_ref[:, cols]
        for tap in range(CONV_WIDTH):
            lo = HALO - (CONV_WIDTH - 1) + tap
            conv = conv + gate[lo:lo + tm, :] * wc_ref[tap:tap + 1, cols]
        half = 0.5 * conv
        mid = ((half + half * jnp.tanh(half)) * val).astype(BF16)
        contrib = _dot(mid, wd_ref[cols, :])
        if c == 0:
            acc_ref[...] = contrib
        else:
            acc_ref[...] += contrib
    y = x_ref[0] + acc_ref[...]
    if final:
        y = _rms(y, fg_ref[...])
    o_ref[0] = y


def _ffn(h, g, w_up, w_conv, b_conv, w_down, final_g=None):
    b, p, d = h.shape
    f = w_down.shape[0]
    fc = 256
    final = final_g is not None
    p_start = BLOCK if final else 0
    tm = _pick_tile(p - p_start, (1024, 832, 640, 512, 256, 128) if final else (832, 640, 512, 256, 128))

    def rows_spec(n_rows, back):
        return pl.BlockSpec(
            (pl.Element(1), pl.Element(n_rows), pl.Element(d)),
            lambda bi, i: (bi, pl.multiple_of(jnp.maximum(p_start + i * tm - back, 0), HALO), 0))

    in_specs = [rows_spec(tm, 0),
                rows_spec(HALO, HALO),
                _const_spec((1, d)),
                _const_spec((d, 2 * f)),
                _const_spec((CONV_WIDTH, f)),
                _const_spec((1, f)),
                _const_spec((f, d))]
    args = [h, h, g.reshape(1, d), w_up, w_conv, b_conv.reshape(1, f), w_down]
    if final:
        in_specs.append(_const_spec((1, d)))
        args.append(final_g.reshape(1, d))
    return pl.pallas_call(
        functools.partial(_ffn_kernel, tm=tm, f=f, fc=fc, final=final, p_start=p_start),
        grid=(b, (p - p_start) // tm),
        in_specs=in_specs,
        out_specs=pl.BlockSpec((1, tm, d), lambda bi, i: (bi, i, 0)),
        out_shape=jax.ShapeDtypeStruct((b, p - p_start, d), F32),
        scratch_shapes=[pltpu.VMEM((tm + HALO, d), BF16), pltpu.VMEM((tm, d), F32)],
        compiler_params=_params(("parallel", "parallel")),
    )(*args)


def _retention_constants():
    c = BLOCK
    hh = np.arange(N_RET_HEADS, dtype=np.float64)
    log_g = np.log1p(-(2.0 ** (-5.0 - hh)))
    j = np.arange(c, dtype=np.float64)
    rel = j[:, None] - j[None, :]
    scale = HEAD_W ** -0.5
    decay = np.where(rel >= 0, np.exp(log_g[:, None, None] * np.maximum(rel, 0.0)), 0.0) * scale
    q_decay = np.broadcast_to(np.exp(log_g[:, None] * (j + 1.0))[:, :, None], (N_RET_HEADS, c, HEAD_W))
    k_decay = np.broadcast_to((np.exp(log_g[:, None] * (c - 1.0 - j)) * scale)[:, :, None],
                              (N_RET_HEADS, c, HEAD_W))
    chunk_decay = tuple(float(v) for v in np.exp(log_g * c))
    return (jnp.asarray(decay, F32), jnp.asarray(q_decay, F32), jnp.asarray(k_decay, F32), chunk_decay)


def _retention_kernel(q_ref, k_ref, v_ref, gate_ref, dmat_ref, qd_ref, kd_ref, gn_ref, o_ref, state_ref,
                      *, tr, chunk_decay):
    i = pl.program_id(1)

    @pl.when(i == 0)
    def _():
        state_ref[...] = jnp.zeros_like(state_ref)

    for c in range(tr // BLOCK):
        rows = slice(c * BLOCK, (c + 1) * BLOCK)
        if c == 0:
            pos = i * tr + lax.broadcasted_iota(jnp.int32, (BLOCK, HEAD_W), 0)
            valid = pos >= N_PAD
        for h in range(N_RET_HEADS):
            cols = slice(h * HEAD_W, (h + 1) * HEAD_W)
            q = q_ref[0, rows, cols]
            k = k_ref[0, rows, cols]
            v = v_ref[0, rows, cols]
            if c == 0:
                k = jnp.where(valid, k, jnp.zeros_like(k))
                v = jnp.where(valid, v, jnp.zeros_like(v))
            scores = _dot_nt(q, k) * dmat_ref[h]
            inner = _dot(scores.astype(BF16), v)
            state = state_ref[h]
            cross = _dot(q, state.astype(BF16)) * qd_ref[h]
            k_scaled = (k.astype(F32) * kd_ref[h]).astype(BF16)
            state_ref[h] = chunk_decay[h] * state + _dot_tn(k_scaled, v)
            o = inner + cross
            oc = o - jnp.mean(o, axis=-1, keepdims=True)
            y = oc * lax.rsqrt(jnp.mean(oc * oc, axis=-1, keepdims=True) + EPS) * gn_ref[:, cols]
            gate = gate_ref[0, rows, cols].astype(F32)
            o_ref[0, rows, cols] = (y * (gate / (1.0 + jnp.exp(-gate)))).astype(BF16)


def _retention(proj, gn):
    b, p, _ = proj.shape
    w = N_RET_HEADS * HEAD_W
    tr = _pick_tile(p, (640, 512, 256, 128))
    dmat, qd, kd, chunk_decay = _retention_constants()

    def col_spec(cb):
        return pl.BlockSpec((1, tr, w), lambda bi, i: (bi, i, cb))

    cshape = (N_RET_HEADS, BLOCK, HEAD_W)
    return pl.pallas_call(
        functools.partial(_retention_kernel, tr=tr, chunk_decay=chunk_decay),
        grid=(b, p // tr),
        in_specs=[col_spec(0), col_spec(1), col_spec(2), col_spec(3),
                  _const_spec(cshape), _const_spec(cshape), _const_spec(cshape), _const_spec((1, w))],
        out_specs=pl.BlockSpec((1, tr, w), lambda bi, i: (bi, i, 0)),
        out_shape=jax.ShapeDtypeStruct((b, p, w), BF16),
        scratch_shapes=[pltpu.VMEM(cshape, F32)],
        compiler_params=_params(("parallel", "arbitrary")),
    )(proj, proj, proj, proj, dmat, qd, kd, gn.reshape(1, w))


def _diff_attn_kernel(slope_ref, lq1_ref, lk1_ref, lq2_ref, lk2_ref, q_ref, k_ref, v_ref, sel_ref, gn_ref, o_ref,
                      qs_ref, m_ref, l_ref, acc_ref, kmax_ref, s0_ref, s1_ref, *, tq, tk, rsub, lambda_init):
    h = pl.program_id(1)
    qi = pl.program_id(2)
    slope = slope_ref[h] * LOG2_E
    q0 = qi * tq
    n_sub = 2 * tq // rsub

    @pl.when(qi == 0)
    def _():
        kf = k_ref[0].astype(F32)
        norms = _dot((kf * kf).astype(BF16), sel_ref[...])
        kmax_ref[...] = jnp.sqrt(jnp.max(norms, axis=0, keepdims=True)) * 1.02

    q = q_ref[0]
    lane = lax.broadcasted_iota(jnp.int32, (tq, HEAD_W), 1)
    qs_ref[0:tq, :] = jnp.where(lane < DIFF_QK_DIM, q, jnp.zeros_like(q))
    qs_ref[tq:, :] = jnp.where(lane >= DIFF_QK_DIM, q, jnp.zeros_like(q))
    m_ref[...] = jnp.full_like(m_ref, MASK_VALUE)
    l_ref[...] = jnp.zeros_like(l_ref)
    acc_ref[...] = jnp.zeros_like(acc_ref)

    def tile_start(j):
        return j * tk if isinstance(j, int) else pl.multiple_of(j * tk, tk)

    s_refs = (s0_ref, s1_ref)

    def issue_scores(j, slot):
        s_refs[slot][...] = _dot_nt(qs_ref[...], k_ref[0, pl.ds(tile_start(j), tk), :])

    def consume(j, slot, mode, prefetch):
        s_ref = s_refs[slot]
        if prefetch is not None:
            issue_scores(prefetch, 1 - slot)
        start = tile_start(j)
        kpos = j * tk + lax.broadcasted_iota(jnp.int32, (1, tk), 1)
        bias = slope * (kpos - q0).astype(F32)
        v = v_ref[0, pl.ds(start, tk), :]

        def stage_b(r):
            rows = slice(r * rsub, (r + 1) * rsub)
            s = s_ref[rows, :] + bias
            if mode == "first":
                s = jnp.where(kpos >= N_PAD, s, MASK_VALUE)
            elif mode == "diag":
                qpos = q0 + (r * rsub) % tq + lax.broadcasted_iota(jnp.int32, (rsub, 1), 0)
                s = jnp.where((qpos >= kpos) & (kpos >= N_PAD), s, MASK_VALUE)
            m_prev = m_ref[rows, :]
            m_new = jnp.maximum(m_prev, jnp.max(s, axis=1, keepdims=True))
            p = jnp.exp2(s - m_new)
            alpha = jnp.exp2(m_prev - m_new)
            l_ref[rows, :] = alpha * l_ref[rows, :] + jnp.sum(p, axis=1, keepdims=True)
            m_ref[rows, :] = m_new
            return p.astype(BF16), alpha

        def stage_c(r, p, alpha):
            rows = slice(r * rsub, (r + 1) * rsub)
            acc_ref[rows, :] = alpha * acc_ref[rows, :] + _dot(p, v)

        b_out = {}
        for t in range(n_sub + 1):
            if t < n_sub:
                b_out[t] = stage_b(t)
            if t >= 1:
                stage_c(t - 1, *b_out.pop(t - 1))

    issue_scores(qi, 0)
    consume(qi, 0, "diag", jnp.maximum(qi - 1, 0))

    qf = qs_ref[...].astype(F32)
    qnorm = jnp.sqrt(jnp.sum(qf * qf, axis=1, keepdims=True))
    row = lax.broadcasted_iota(jnp.int32, (2 * tq, 1), 0)
    kmax = jnp.where(row < tq, kmax_ref[:, 0:1], kmax_ref[:, 1:2])
    slack = jnp.max(qnorm * kmax - m_ref[...])

    def needed(rel):
        return slack + slope * rel >= -ZERO_PROB_LOG2

    def cond(carry):
        j, rel, _ = carry
        return (j >= 1) & needed(rel)

    def body(carry):
        j, rel, slot = carry
        for static_slot in (0, 1):
            @pl.when(slot == static_slot)
            def _():
                consume(j, static_slot, "plain", j - 1)
        return j - 1, rel - tk, 1 - slot

    j_end, rel_end, slot_end = lax.while_loop(cond, body, (qi - 1, jnp.float32(-1.0), jnp.int32(1)))

    for static_slot in (0, 1):
        @pl.when((j_end == 0) & needed(rel_end) & (slot_end == static_slot))
        def _():
            consume(0, static_slot, "first", None)

    o = acc_ref[...] / l_ref[...]
    lam = (jnp.exp(jnp.sum(lq1_ref[...] * lk1_ref[...], axis=1, keepdims=True))
           - jnp.exp(jnp.sum(lq2_ref[...] * lk2_ref[...], axis=1, keepdims=True)) + lambda_init)
    o = o[0:tq, :] - lam * o[tq:, :]
    o_ref[0] = (_rms(o, gn_ref[...]) * (1.0 - lambda_init)).astype(BF16)


def _diff_attention(proj, lam_q1, lam_k1, lam_q2, lam_k2, gn, lambda_init, col0):
    b, p, _ = proj.shape
    nh = N_DIFF_HEADS
    tq = tk = _pick_tile(p, (640, 512, 256, 128))
    rsub = tq
    slopes = jnp.asarray(2.0 ** (-8.0 * (np.arange(nh) + 1.0) / nh), F32)
    lam_spec = _const_spec((1, DIFF_QK_DIM))
    kv_block = (1, p, HEAD_W)
    sel = np.zeros((HEAD_W, HEAD_W), np.float32)
    sel[:DIFF_QK_DIM, 0] = 1.0
    sel[DIFF_QK_DIM:, 1] = 1.0
    return pl.pallas_call(
        functools.partial(_diff_attn_kernel, tq=tq, tk=tk, rsub=rsub, lambda_init=lambda_init),
        grid=(b, nh, p // tq),
        in_specs=[pl.BlockSpec(memory_space=pltpu.SMEM),
                  lam_spec, lam_spec, lam_spec, lam_spec,
                  pl.BlockSpec((1, tq, HEAD_W), lambda bi, h, i: (bi, i, col0 + h)),
                  pl.BlockSpec(kv_block, lambda bi, h, i: (bi, 0, col0 + nh + h)),
                  pl.BlockSpec(kv_block, lambda bi, h, i: (bi, 0, col0 + 2 * nh + h)),
                  _const_spec((HEAD_W, HEAD_W)),
                  pl.BlockSpec((1, HEAD_W), lambda bi, h, i: (0, h))],
        out_specs=pl.BlockSpec((1, tq, HEAD_W), lambda bi, h, i: (bi, i, h)),
        out_shape=jax.ShapeDtypeStruct((b, p, nh * HEAD_W), BF16),
        scratch_shapes=[pltpu.VMEM((2 * tq, HEAD_W), BF16),
                        pltpu.VMEM((2 * tq, 1), F32),
                        pltpu.VMEM((2 * tq, 1), F32),
                        pltpu.VMEM((2 * tq, HEAD_W), F32),
                        pltpu.VMEM((1, HEAD_W), F32),
                        pltpu.VMEM((2 * tq, tk), F32),
                        pltpu.VMEM((2 * tq, tk), F32)],
        compiler_params=_params(("parallel", "parallel", "arbitrary")),
    )(slopes, lam_q1.reshape(1, -1), lam_k1.reshape(1, -1), lam_q2.reshape(1, -1), lam_k2.reshape(1, -1),
      proj, proj, proj, jnp.asarray(sel, BF16), gn.reshape(1, -1))


def _cumsum_blocks(tk):
    blocks, start = [], 0
    while start < tk:
        width = 256 if tk - start >= 256 else tk - start
        blocks.append((start, width))
        start += width
    return tuple(blocks)


def _sb_attn_kernel(q_ref, k_ref, v_ref, u_ref, o_ref, run_ref, acc_ref, *, tq, heads):
    qi = pl.program_id(2)
    q0 = pl.multiple_of(qi * tq, BLOCK)

    def attend(pieces):
        jobs = [(hd,) + tuple(piece) for piece in pieces for hd in range(heads)]
        n = len(jobs)
        z, softplus, log_beta, mask, later, w = ([None] * n for _ in range(6))
        for i, (hd, row_lo, n_rows, kstart, width, masked, first) in enumerate(jobs):
            cols = slice(hd * HEAD_W, (hd + 1) * HEAD_W)
            z[i] = _dot_nt(q_ref[0, row_lo:row_lo + n_rows, cols], k_ref[0, pl.ds(kstart, width), cols])
        for i, (hd, row_lo, n_rows, kstart, width, masked, first) in enumerate(jobs):
            neg_abs = lax.bitcast_convert_type(
                lax.bitcast_convert_type(z[i], jnp.uint32) | jnp.uint32(0x80000000), F32)
            sp = jnp.maximum(z[i], 0.0) + jnp.log(1.0 + jnp.exp2(neg_abs)) * INV_LN2
            log_beta[i] = z[i] - sp
            if masked:
                kpos = kstart + lax.broadcasted_iota(jnp.int32, (1, width), 1)
                if masked == "pad":
                    mask[i] = jnp.broadcast_to(kpos >= N_PAD, sp.shape)
                else:
                    qpos = q0 + row_lo + lax.broadcasted_iota(jnp.int32, (n_rows, 1), 0)
                    mask[i] = (qpos > kpos) if masked == "causal" else (qpos > kpos) & (kpos >= N_PAD)
                sp = jnp.where(mask[i], sp, 0.0)
            softplus[i] = sp
        for i, (hd, row_lo, n_rows, kstart, width, masked, first) in enumerate(jobs):
            rows = slice(row_lo, row_lo + n_rows)
            run = jnp.zeros((n_rows, 1), F32) if first else run_ref[hd, rows, :]
            sp16 = softplus[i].astype(BF16)
            blocks = _cumsum_blocks(width)
            parts = [None] * len(blocks)
            for idx in reversed(range(len(blocks))):
                b0, bw = blocks[idx]
                parts[idx] = _dot(sp16[:, b0:b0 + bw], u_ref[0:bw, 0:bw]) + run
                run = run - jnp.sum(softplus[i][:, b0:b0 + bw], axis=1, keepdims=True)
            run_ref[hd, rows, :] = run
            later[i] = parts[0] if len(parts) == 1 else jnp.concatenate(parts, axis=1)
        for i, (hd, row_lo, n_rows, kstart, width, masked, first) in enumerate(jobs):
            w[i] = jnp.exp2(log_beta[i] + later[i])
            if masked:
                w[i] = jnp.where(mask[i], w[i], 0.0)
            w[i] = w[i].astype(BF16)
        for i, (hd, row_lo, n_rows, kstart, width, masked, first) in enumerate(jobs):
            rows = slice(row_lo, row_lo + n_rows)
            cols = slice(hd * HEAD_W, (hd + 1) * HEAD_W)
            contrib = _dot(w[i], v_ref[0, pl.ds(kstart, width), cols])
            if first:
                acc_ref[rows, cols] = contrib
            else:
                acc_ref[rows, cols] += contrib

    def before_tile(row_lo, n_rows, kstart, width, masked):
        return [(row_lo + r, BLOCK, kstart, width, masked, False) for r in range(0, n_rows, BLOCK)]

    top = min(SB_TOP_ROWS, tq)
    n_blk = tq // BLOCK

    @pl.when(qi == 0)
    def _():
        attend([(r * BLOCK, BLOCK, q0, (r + 1) * BLOCK, "both", True) for r in range(n_blk)])

    def own(blocks):
        diag = [(r * BLOCK, BLOCK, pl.multiple_of(q0 + r * BLOCK, BLOCK), BLOCK, "causal", True) for r in blocks]
        rest = [(r * BLOCK, BLOCK, q0, r * BLOCK, False, False) for r in blocks if r > 0]
        return diag + rest

    @pl.when(qi > 0)
    def _():
        attend(own(range(top // BLOCK)))
        attend(own(range(top // BLOCK, n_blk))
               + before_tile(0, top, pl.multiple_of(q0 - SB_KEY_CHUNK, BLOCK), SB_KEY_CHUNK, False))

    def walk(row_lo, n_rows, chunks_done):
        def more_needed():
            return jnp.max(run_ref[:, row_lo:row_lo + n_rows, :]) > -ZERO_PROB_LOG2

        def chunk(kstart, width, masked):
            attend(before_tile(row_lo, n_rows, pl.multiple_of(kstart, BLOCK), width, masked))

        def cond(carry):
            kstart, go = carry
            return (kstart >= BLOCK) & go

        def body(carry):
            kstart, _ = carry
            chunk(kstart, SB_KEY_CHUNK, False)
            return kstart - SB_KEY_CHUNK, more_needed()

        kstart, go = lax.while_loop(cond, body, (q0 - (chunks_done + 1) * SB_KEY_CHUNK, more_needed()))

        def cond_tail(carry):
            kstart, go = carry
            return (kstart >= 0) & go

        def body_tail(carry):
            kstart, _ = carry
            chunk(kstart, BLOCK, "pad")
            return kstart - BLOCK, more_needed()

        lax.while_loop(cond_tail, body_tail, (kstart + SB_KEY_CHUNK - BLOCK, go))

    walk(0, top, 1)
    if top < tq:
        @pl.when(jnp.max(run_ref[:, top:, :]) > -ZERO_PROB_LOG2)
        def _():
            walk(top, tq - top, 0)

    o_ref[0] = acc_ref[...].astype(BF16)


def _sb_attention(qkv):
    b, p, _ = qkv.shape
    nh = N_SB_HEADS
    tq = _pick_tile(p, (640, 512, 256, 128))
    umax = SB_KEY_CHUNK
    jj = np.arange(umax)
    u = jnp.asarray(-(jj[:, None] > jj[None, :]).astype(np.float32), BF16)
    heads = SB_HEADS_PER_STEP
    groups = nh // heads
    gw = heads * HEAD_W
    kv_block = (1, p, gw)
    return pl.pallas_call(
        functools.partial(_sb_attn_kernel, tq=tq, heads=heads),
        grid=(b, groups, p // tq),
        in_specs=[pl.BlockSpec((1, tq, gw), lambda bi, g, i: (bi, i, g)),
                  pl.BlockSpec(kv_block, lambda bi, g, i: (bi, 0, groups + g)),
                  pl.BlockSpec(kv_block, lambda bi, g, i: (bi, 0, 2 * groups + g)),
                  _const_spec((umax, umax))],
        out_specs=pl.BlockSpec((1, tq, gw), lambda bi, g, i: (bi, i, g)),
        out_shape=jax.ShapeDtypeStruct((b, p, nh * HEAD_W), BF16),
        scratch_shapes=[pltpu.VMEM((heads, tq, 1), F32), pltpu.VMEM((tq, gw), F32)],
        compiler_params=_params(("parallel", "parallel", "arbitrary")),
    )(qkv, qkv, qkv, u)


def kernel(x, meta_tokens, mix_norm, ffn_norm, ffn_up, ffn_conv, ffn_conv_b, ffn_down, ab_w_in, ab_ret_norm,
           ab_diff_norm, ab_lam_q1, ab_lam_k1, ab_lam_q2, ab_lam_k2, ab_w_out, c_w_in, c_w_out, final_norm):
    b, s, d = x.shape
    depth = mix_norm.shape[0]
    p = s + BLOCK
    t = b * p
    ret_w = N_RET_HEADS * HEAD_W
    h = None
    for i in range(depth):
        if i % 2 == 0:
            e = i // 2
            lambda_init = 0.8 - 0.6 * math.exp(-0.3 * i)
            w_in = ab_w_in[e].astype(BF16)
            scales = (1.0, 1.0, 1.0, 1.0, DIFF_QK_DIM ** -0.5 * LOG2_E, 1.0, 1.0)
            if i == 0:
                proj, h = _embed_norm_proj(x, meta_tokens, mix_norm[i], w_in, scales, 512)
            else:
                proj = _norm_proj(h.reshape(t, d), mix_norm[i], w_in, scales, 512).reshape(b, p, -1)
            ret = _retention(proj, ab_ret_norm[e])
            dif = _diff_attention(proj, ab_lam_q1[e], ab_lam_k1[e], ab_lam_q2[e], ab_lam_k2[e],
                                  ab_diff_norm[e], lambda_init, 4 * N_RET_HEADS)
            w_out = ab_w_out[e].astype(BF16)
            h = _res_proj(h.reshape(t, d), [ret.reshape(t, -1), dif.reshape(t, -1)],
                          [w_out[:ret_w], w_out[ret_w:]]).reshape(b, p, d)
        else:
            o = i // 2
            w_in = c_w_in[o].astype(BF16)
            n_chunks = w_in.shape[1] // 512
            q_chunks = N_SB_HEADS * HEAD_W // 512
            scales = tuple(HEAD_W ** -0.5 * LOG2_E if c < q_chunks else 1.0 for c in range(n_chunks))
            qkv = _norm_proj(h.reshape(t, d), mix_norm[i], w_in, scales, 512).reshape(b, p, -1)
            att = _sb_attention(qkv)
            h = _res_proj(h.reshape(t, d), [att.reshape(t, -1)], [c_w_out[o].astype(BF16)]).reshape(b, p, d)
        h = _ffn(h, ffn_norm[i], ffn_up[i].astype(BF16), ffn_conv[i], ffn_conv_b[i], ffn_down[i].astype(BF16),
                 final_norm if i == depth - 1 else None)
    return h
```

```python
import functools
import math

import numpy as np
import jax
import jax.numpy as jnp
from jax import lax
from jax.experimental import pallas as pl
from jax.experimental.pallas import tpu as pltpu

F32 = jnp.float32
BF16 = jnp.bfloat16

N_META = 16
BLOCK = 128
N_PAD = BLOCK - N_META
EPS = 1e-6
MASK_VALUE = -1e30
HEAD_W = 128
N_RET_HEADS = 4
N_DIFF_HEADS = 4
DIFF_QK_DIM = 64
N_SB_HEADS = 8
CONV_WIDTH = 3
HALO = 16
FFN_CHUNK = 768
LOG2_E = math.log2(math.e)
INV_LN2 = 1.0 / math.log(2.0)
ZERO_PROB_LOG2 = 160.0
SB_KEY_CHUNK = 256
SB_TOP_ROWS = 256
SB_HEADS_PER_STEP = 2

VMEM_LIMIT_BYTES = 56 * 1024 * 1024


def _pick_tile(n, candidates):
    for c in candidates:
        if n % c == 0:
            return c
    raise ValueError(f"no tile for {n} in {candidates}")


def _params(sem):
    return pltpu.CompilerParams(dimension_semantics=sem, vmem_limit_bytes=VMEM_LIMIT_BYTES)


def _const_spec(shape):
    return pl.BlockSpec(shape, lambda *_: (0,) * len(shape), pipeline_mode=pl.Buffered(1))


def _rms(x, g):
    return x * lax.rsqrt(jnp.mean(x * x, axis=-1, keepdims=True) + EPS) * g


def _dot(a, b):
    return jnp.dot(a, b, preferred_element_type=F32)


def _dot_nt(a, b):
    return lax.dot_general(a, b, (((1,), (1,)), ((), ())), preferred_element_type=F32)


def _dot_tn(a, b):
    return lax.dot_general(a, b, (((0,), (0,)), ((), ())), preferred_element_type=F32)


def _norm_proj_kernel(x_ref, g_ref, w_ref, o_ref, xn_ref, *, chunk, scales):
    xn_ref[...] = _rms(x_ref[...], g_ref[...]).astype(BF16)
    for c, scale in enumerate(scales):
        cols = slice(c * chunk, (c + 1) * chunk)
        r = _dot(xn_ref[...], w_ref[:, cols])
        if scale != 1.0:
            r = r * scale
        o_ref[:, cols] = r.astype(BF16)


def _norm_proj(x2, g, w, scales, chunk):
    t, d = x2.shape
    n = w.shape[1]
    tm = _pick_tile(t, (1024, 640, 512, 256, 128))
    return pl.pallas_call(
        functools.partial(_norm_proj_kernel, chunk=chunk, scales=scales),
        grid=(t // tm,),
        in_specs=[pl.BlockSpec((tm, d), lambda i: (i, 0)),
                  _const_spec((1, d)),
                  _const_spec((d, n))],
        out_specs=pl.BlockSpec((tm, n), lambda i: (i, 0)),
        out_shape=jax.ShapeDtypeStruct((t, n), BF16),
        scratch_shapes=[pltpu.VMEM((tm, d), BF16)],
        compiler_params=_params(("parallel",)),
    )(x2, g.reshape(1, d), w)


def _embed_norm_proj_kernel(x_ref, lead_ref, g_ref, w_ref, o_ref, h_ref, xn_ref, *, chunk, scales):
    tm = h_ref.shape[1]

    @pl.when(pl.program_id(1) == 0)
    def _():
        h_ref[0, 0:BLOCK, :] = lead_ref[...]
        h_ref[0, BLOCK:, :] = x_ref[0, 0:tm - BLOCK, :]

    @pl.when(pl.program_id(1) > 0)
    def _():
        h_ref[0] = x_ref[0]

    xn_ref[...] = _rms(h_ref[0], g_ref[...]).astype(BF16)
    for c, scale in enumerate(scales):
        cols = slice(c * chunk, (c + 1) * chunk)
        r = _dot(xn_ref[...], w_ref[:, cols])
        if scale != 1.0:
            r = r * scale
        o_ref[0, :, cols] = r.astype(BF16)


def _embed_norm_proj(x, meta_tokens, g, w, scales, chunk):
    b, s, d = x.shape
    p = s + BLOCK
    n = w.shape[1]
    tm = _pick_tile(p, (640, 512, 256, 128))
    lead = jnp.concatenate([jnp.zeros((N_PAD, d), x.dtype), meta_tokens.astype(x.dtype)], axis=0)
    x_spec = pl.BlockSpec((pl.Element(1), pl.Element(tm), pl.Element(d)),
                          lambda bi, i: (bi, pl.multiple_of(jnp.maximum(i * tm - BLOCK, 0), BLOCK), 0))
    return pl.pallas_call(
        functools.partial(_embed_norm_proj_kernel, chunk=chunk, scales=scales),
        grid=(b, p // tm),
        in_specs=[x_spec, _const_spec((BLOCK, d)), _const_spec((1, d)), _const_spec((d, n))],
        out_specs=[pl.BlockSpec((1, tm, n), lambda bi, i: (bi, i, 0)),
                   pl.BlockSpec((1, tm, d), lambda bi, i: (bi, i, 0))],
        out_shape=[jax.ShapeDtypeStruct((b, p, n), BF16), jax.ShapeDtypeStruct((b, p, d), F32)],
        scratch_shapes=[pltpu.VMEM((tm, d), BF16)],
        compiler_params=_params(("parallel", "parallel")),
    )(x, lead, g.reshape(1, d), w)


def _ffn_kernel(*refs, tm, f, bounds, final, p_start, act_widths):
    n_act = len(act_widths)
    x_ref, xh_ref = refs[0:2]
    act_refs = refs[2:2 + 2 * n_act]
    pos = 2 + 2 * n_act
    if n_act:
        wo_ref = refs[pos]
        pos += 1
    g_ref, wup_ref, wc_ref, bc_ref, wd_ref = refs[pos:pos + 5]
    pos += 5
    if final:
        fg_ref = refs[pos]
        pos += 1
    o_ref, xn_ref, acc_ref = refs[pos:pos + 3]
    i = pl.program_id(1)
    g = g_ref[...]
    if n_act:
        a_ref = refs[pos + 3]
        col = 0
        for k, width in enumerate(act_widths):
            a_ref[0:HALO, col:col + width] = act_refs[2 * k + 1][0]
            a_ref[HALO:, col:col + width] = act_refs[2 * k][0]
            col += width
        mix = _dot(a_ref[...], wo_ref[...])
        x_halo = xh_ref[0] + mix[0:HALO, :]
        o_ref[0] = x_ref[0] + mix[HALO:, :]
    else:
        x_halo = xh_ref[0]
        o_ref[0] = x_ref[0]
    xn_ref[0:HALO, :] = _rms(x_halo, g).astype(BF16)
    xn_ref[HALO:, :] = _rms(o_ref[0], g).astype(BF16)
    if p_start < N_PAD + HALO:
        masked_rows = HALO + N_PAD - p_start

        @pl.when(i == 0)
        def _():
            xn_ref[0:masked_rows, :] = jnp.zeros((masked_rows, xn_ref.shape[1]), BF16)

    def up_proj(c):
        lo, hi = bounds[c]
        gate = _dot(xn_ref[...], wup_ref[:, lo:hi])
        val = _dot(xn_ref[HALO:, :], wup_ref[:, f + lo:f + hi])
        return gate, val

    n_chunks = len(bounds)
    nxt = up_proj(0)
    for c in range(n_chunks):
        cols = slice(*bounds[c])
        gate, val = nxt
        if c + 1 < n_chunks:
            nxt = up_proj(c + 1)
        conv = bc_ref[:, cols]
        for tap in range(CONV_WIDTH):
            lo = HALO - (CONV_WIDTH - 1) + tap
            conv = conv + gate[lo:lo + tm, :] * wc_ref[tap:tap + 1, cols]
        half = 0.5 * conv
        mid = ((half + half * jnp.tanh(half)) * val).astype(BF16)
        contrib = _dot(mid, wd_ref[cols, :])
        if c == 0:
            acc_ref[...] = contrib
        else:
            acc_ref[...] += contrib
    y = o_ref[0] + acc_ref[...]
    if final:
        y = _rms(y, fg_ref[...])
    o_ref[0] = y


def _mixer_out_ffn(h, acts, w_out, g, w_up, w_conv, b_conv, w_down, final_g=None):
    b, p, d = h.shape
    f = w_down.shape[0]
    final = final_g is not None
    p_start = BLOCK if final else 0
    tm = _pick_tile(p - p_start, (1024, 832, 640, 512, 256, 128) if final else (832, 640, 512, 256, 128))
    bounds = tuple((lo, min(lo + FFN_CHUNK, f)) for lo in range(0, f, FFN_CHUNK))
    act_widths = tuple(a.shape[-1] for a in acts)

    def rows_spec(n_rows, back, width):
        return pl.BlockSpec(
            (pl.Element(1), pl.Element(n_rows), pl.Element(width)),
            lambda bi, i: (bi, pl.multiple_of(jnp.maximum(p_start + i * tm - back, 0), HALO), 0))

    in_specs = [rows_spec(tm, 0, d), rows_spec(HALO, HALO, d)]
    args = [h, h]
    for a, width in zip(acts, act_widths):
        in_specs += [rows_spec(tm, 0, width), rows_spec(HALO, HALO, width)]
        args += [a, a]
    in_specs.append(_const_spec(w_out.shape))
    args.append(w_out)
    in_specs += [_const_spec((1, d)),
                 _const_spec((d, 2 * f)),
                 _const_spec((CONV_WIDTH, f)),
                 _const_spec((1, f)),
                 _const_spec((f, d))]
    args += [g.reshape(1, d), w_up, w_conv, b_conv.reshape(1, f), w_down]
    if final:
        in_specs.append(_const_spec((1, d)))
        args.append(final_g.reshape(1, d))
    return pl.pallas_call(
        functools.partial(_ffn_kernel, tm=tm, f=f, bounds=bounds, final=final, p_start=p_start,
                          act_widths=act_widths),
        grid=(b, (p - p_start) // tm),
        in_specs=in_specs,
        out_specs=pl.BlockSpec((1, tm, d), lambda bi, i: (bi, i, 0)),
        out_shape=jax.ShapeDtypeStruct((b, p - p_start, d), F32),
        scratch_shapes=[pltpu.VMEM((tm + HALO, d), BF16), pltpu.VMEM((tm, d), F32),
                        pltpu.VMEM((tm + HALO, sum(act_widths)), BF16)],
        compiler_params=_params(("parallel", "parallel")),
    )(*args)


def _retention_constants():
    c = BLOCK
    hh = np.arange(N_RET_HEADS, dtype=np.float64)
    log_g = np.log1p(-(2.0 ** (-5.0 - hh)))
    j = np.arange(c, dtype=np.float64)
    rel = j[:, None] - j[None, :]
    scale = HEAD_W ** -0.5
    decay = np.where(rel >= 0, np.exp(log_g[:, None, None] * np.maximum(rel, 0.0)), 0.0) * scale
    q_decay = np.broadcast_to(np.exp(log_g[:, None] * (j + 1.0))[:, :, None], (N_RET_HEADS, c, HEAD_W))
    k_decay = np.broadcast_to((np.exp(log_g[:, None] * (c - 1.0 - j)) * scale)[:, :, None],
                              (N_RET_HEADS, c, HEAD_W))
    chunk_decay = tuple(float(v) for v in np.exp(log_g * c))
    return (jnp.asarray(decay, F32), jnp.asarray(q_decay, F32), jnp.asarray(k_decay, F32), chunk_decay)


def _retention_kernel(q_ref, k_ref, v_ref, gate_ref, dmat_ref, qd_ref, kd_ref, gn_ref, o_ref, state_ref,
                      *, tr, chunk_decay):
    i = pl.program_id(1)

    @pl.when(i == 0)
    def _():
        state_ref[...] = jnp.zeros_like(state_ref)

    for c in range(tr // BLOCK):
        rows = slice(c * BLOCK, (c + 1) * BLOCK)
        if c == 0:
            pos = i * tr + lax.broadcasted_iota(jnp.int32, (BLOCK, HEAD_W), 0)
            valid = pos >= N_PAD
        for h in range(N_RET_HEADS):
            cols = slice(h * HEAD_W, (h + 1) * HEAD_W)
            q = q_ref[0, rows, cols]
            k = k_ref[0, rows, cols]
            v = v_ref[0, rows, cols]
            if c == 0:
                k = jnp.where(valid, k, jnp.zeros_like(k))
                v = jnp.where(valid, v, jnp.zeros_like(v))
            scores = _dot_nt(q, k) * dmat_ref[h]
            inner = _dot(scores.astype(BF16), v)
            state = state_ref[h]
            cross = _dot(q, state.astype(BF16)) * qd_ref[h]
            k_scaled = (k.astype(F32) * kd_ref[h]).astype(BF16)
            state_ref[h] = chunk_decay[h] * state + _dot_tn(k_scaled, v)
            o = inner + cross
            oc = o - jnp.mean(o, axis=-1, keepdims=True)
            y = oc * lax.rsqrt(jnp.mean(oc * oc, axis=-1, keepdims=True) + EPS) * gn_ref[:, cols]
            gate = gate_ref[0, rows, cols].astype(F32)
            o_ref[0, rows, cols] = (y * (gate / (1.0 + jnp.exp(-gate)))).astype(BF16)


def _retention(proj, gn):
    b, p, _ = proj.shape
    w = N_RET_HEADS * HEAD_W
    tr = _pick_tile(p, (640, 512, 256, 128))
    dmat, qd, kd, chunk_decay = _retention_constants()

    def col_spec(cb):
        return pl.BlockSpec((1, tr, w), lambda bi, i: (bi, i, cb))

    cshape = (N_RET_HEADS, BLOCK, HEAD_W)
    return pl.pallas_call(
        functools.partial(_retention_kernel, tr=tr, chunk_decay=chunk_decay),
        grid=(b, p // tr),
        in_specs=[col_spec(0), col_spec(1), col_spec(2), col_spec(3),
                  _const_spec(cshape), _const_spec(cshape), _const_spec(cshape), _const_spec((1, w))],
        out_specs=pl.BlockSpec((1, tr, w), lambda bi, i: (bi, i, 0)),
        out_shape=jax.ShapeDtypeStruct((b, p, w), BF16),
        scratch_shapes=[pltpu.VMEM(cshape, F32)],
        compiler_params=_params(("parallel", "arbitrary")),
    )(proj, proj, proj, proj, dmat, qd, kd, gn.reshape(1, w))


def _diff_attn_kernel(slope_ref, lq1_ref, lk1_ref, lq2_ref, lk2_ref, q_ref, k_ref, v_ref, sel_ref, gn_ref, o_ref,
                      qs_ref, m_ref, l_ref, acc_ref, kmax_ref, s0_ref, s1_ref, *, tq, tk, rsub, lambda_init):
    h = pl.program_id(1)
    qi = pl.program_id(2)
    slope = slope_ref[h] * LOG2_E
    q0 = qi * tq
    n_sub = 2 * tq // rsub

    @pl.when(qi == 0)
    def _():
        kf = k_ref[0].astype(F32)
        norms = _dot((kf * kf).astype(BF16), sel_ref[...])
        kmax_ref[...] = jnp.sqrt(jnp.max(norms, axis=0, keepdims=True)) * 1.02

    q = q_ref[0]
    lane = lax.broadcasted_iota(jnp.int32, (tq, HEAD_W), 1)
    qs_ref[0:tq, :] = jnp.where(lane < DIFF_QK_DIM, q, jnp.zeros_like(q))
    qs_ref[tq:, :] = jnp.where(lane >= DIFF_QK_DIM, q, jnp.zeros_like(q))
    m_ref[...] = jnp.full_like(m_ref, MASK_VALUE)
    l_ref[...] = jnp.zeros_like(l_ref)
    acc_ref[...] = jnp.zeros_like(acc_ref)

    def tile_start(j):
        return j * tk if isinstance(j, int) else pl.multiple_of(j * tk, tk)

    s_refs = (s0_ref, s1_ref)

    def issue_scores(j, slot):
        s_refs[slot][...] = _dot_nt(qs_ref[...], k_ref[0, pl.ds(tile_start(j), tk), :])

    def consume(j, slot, mode, prefetch):
        s_ref = s_refs[slot]
        if prefetch is not None:
            issue_scores(prefetch, 1 - slot)
        start = tile_start(j)
        kpos = j * tk + lax.broadcasted_iota(jnp.int32, (1, tk), 1)
        bias = slope * (kpos - q0).astype(F32)
        v = v_ref[0, pl.ds(start, tk), :]

        def stage_b(r):
            rows = slice(r * rsub, (r + 1) * rsub)
            s = s_ref[rows, :] + bias
            if mode == "first":
                s = jnp.where(kpos >= N_PAD, s, MASK_VALUE)
            elif mode == "diag":
                qpos = q0 + (r * rsub) % tq + lax.broadcasted_iota(jnp.int32, (rsub, 1), 0)
                s = jnp.where((qpos >= kpos) & (kpos >= N_PAD), s, MASK_VALUE)
            m_prev = m_ref[rows, :]
            m_new = jnp.maximum(m_prev, jnp.max(s, axis=1, keepdims=True))
            p = jnp.exp2(s - m_new)
            alpha = jnp.exp2(m_prev - m_new)
            l_ref[rows, :] = alpha * l_ref[rows, :] + jnp.sum(p, axis=1, keepdims=True)
            m_ref[rows, :] = m_new
            return p.astype(BF16), alpha

        def stage_c(r, p, alpha):
            rows = slice(r * rsub, (r + 1) * rsub)
            acc_ref[rows, :] = alpha * acc_ref[rows, :] + _dot(p, v)

        b_out = {}
        for t in range(n_sub + 1):
            if t < n_sub:
                b_out[t] = stage_b(t)
            if t >= 1:
                stage_c(t - 1, *b_out.pop(t - 1))

    issue_scores(qi, 0)
    consume(qi, 0, "diag", jnp.maximum(qi - 1, 0))

    qf = qs_ref[...].astype(F32)
    qnorm = jnp.sqrt(jnp.sum(qf * qf, axis=1, keepdims=True))
    row = lax.broadcasted_iota(jnp.int32, (2 * tq, 1), 0)
    kmax = jnp.where(row < tq, kmax_ref[:, 0:1], kmax_ref[:, 1:2])
    slack = jnp.max(qnorm * kmax - m_ref[...])

    def needed(rel):
        return slack + slope * rel >= -ZERO_PROB_LOG2

    def cond(carry):
        j, rel, _ = carry
        return (j >= 1) & needed(rel)

    def body(carry):
        j, rel, slot = carry
        for static_slot in (0, 1):
            @pl.when(slot == static_slot)
            def _():
                consume(j, static_slot, "plain", j - 1)
        return j - 1, rel - tk, 1 - slot

    j_end, rel_end, slot_end = lax.while_loop(cond, body, (qi - 1, jnp.float32(-1.0), jnp.int32(1)))

    for static_slot in (0, 1):
        @pl.when((j_end == 0) & needed(rel_end) & (slot_end == static_slot))
        def _():
            consume(0, static_slot, "first", None)

    o = acc_ref[...] / l_ref[...]
    lam = (jnp.exp(jnp.sum(lq1_ref[...] * lk1_ref[...], axis=1, keepdims=True))
           - jnp.exp(jnp.sum(lq2_ref[...] * lk2_ref[...], axis=1, keepdims=True)) + lambda_init)
    o = o[0:tq, :] - lam * o[tq:, :]
    o_ref[0] = (_rms(o, gn_ref[...]) * (1.0 - lambda_init)).astype(BF16)


def _diff_attention(proj, lam_q1, lam_k1, lam_q2, lam_k2, gn, lambda_init, col0):
    b, p, _ = proj.shape
    nh = N_DIFF_HEADS
    tq = tk = _pick_tile(p, (640, 512, 256, 128))
    rsub = tq
    slopes = jnp.asarray(2.0 ** (-8.0 * (np.arange(nh) + 1.0) / nh), F32)
    lam_spec = _const_spec((1, DIFF_QK_DIM))
    kv_block = (1, p, HEAD_W)
    sel = np.zeros((HEAD_W, HEAD_W), np.float32)
    sel[:DIFF_QK_DIM, 0] = 1.0
    sel[DIFF_QK_DIM:, 1] = 1.0
    return pl.pallas_call(
        functools.partial(_diff_attn_kernel, tq=tq, tk=tk, rsub=rsub, lambda_init=lambda_init),
        grid=(b, nh, p // tq),
        in_specs=[pl.BlockSpec(memory_space=pltpu.SMEM),
                  lam_spec, lam_spec, lam_spec, lam_spec,
                  pl.BlockSpec((1, tq, HEAD_W), lambda bi, h, i: (bi, i, col0 + h)),
                  pl.BlockSpec(kv_block, lambda bi, h, i: (bi, 0, col0 + nh + h)),
                  pl.BlockSpec(kv_block, lambda bi, h, i: (bi, 0, col0 + 2 * nh + h)),
                  _const_spec((HEAD_W, HEAD_W)),
                  pl.BlockSpec((1, HEAD_W), lambda bi, h, i: (0, h))],
        out_specs=pl.BlockSpec((1, tq, HEAD_W), lambda bi, h, i: (bi, i, h)),
        out_shape=jax.ShapeDtypeStruct((b, p, nh * HEAD_W), BF16),
        scratch_shapes=[pltpu.VMEM((2 * tq, HEAD_W), BF16),
                        pltpu.VMEM((2 * tq, 1), F32),
                        pltpu.VMEM((2 * tq, 1), F32),
                        pltpu.VMEM((2 * tq, HEAD_W), F32),
                        pltpu.VMEM((1, HEAD_W), F32),
                        pltpu.VMEM((2 * tq, tk), F32),
                        pltpu.VMEM((2 * tq, tk), F32)],
        compiler_params=_params(("parallel", "parallel", "arbitrary")),
    )(slopes, lam_q1.reshape(1, -1), lam_k1.reshape(1, -1), lam_q2.reshape(1, -1), lam_k2.reshape(1, -1),
      proj, proj, proj, jnp.asarray(sel, BF16), gn.reshape(1, -1))


def _cumsum_blocks(tk):
    blocks, start = [], 0
    while start < tk:
        width = 256 if tk - start >= 256 else tk - start
        blocks.append((start, width))
        start += width
    return tuple(blocks)


def _sb_attn_kernel(q_ref, k_ref, v_ref, u_ref, o_ref, run_ref, acc_ref, *, tq, heads):
    qi = pl.program_id(2)
    q0 = pl.multiple_of(qi * tq, BLOCK)

    def attend(pieces):
        jobs = [(hd,) + tuple(piece) for piece in pieces for hd in range(heads)]
        n = len(jobs)
        z, softplus, log_beta, mask, later, w = ([None] * n for _ in range(6))
        for i, (hd, row_lo, n_rows, kstart, width, masked, first) in enumerate(jobs):
            cols = slice(hd * HEAD_W, (hd + 1) * HEAD_W)
            z[i] = _dot_nt(q_ref[0, row_lo:row_lo + n_rows, cols], k_ref[0, pl.ds(kstart, width), cols])
        for i, (hd, row_lo, n_rows, kstart, width, masked, first) in enumerate(jobs):
            neg_abs = lax.bitcast_convert_type(
                lax.bitcast_convert_type(z[i], jnp.uint32) | jnp.uint32(0x80000000), F32)
            sp = jnp.maximum(z[i], 0.0) + jnp.log(1.0 + jnp.exp2(neg_abs)) * INV_LN2
            log_beta[i] = z[i] - sp
            if masked:
                kpos = kstart + lax.broadcasted_iota(jnp.int32, (1, width), 1)
                if masked == "pad":
                    mask[i] = jnp.broadcast_to(kpos >= N_PAD, sp.shape)
                else:
                    qpos = q0 + row_lo + lax.broadcasted_iota(jnp.int32, (n_rows, 1), 0)
                    mask[i] = (qpos > kpos) if masked == "causal" else (qpos > kpos) & (kpos >= N_PAD)
                sp = jnp.where(mask[i], sp, 0.0)
            softplus[i] = sp
        for i, (hd, row_lo, n_rows, kstart, width, masked, first) in enumerate(jobs):
            rows = slice(row_lo, row_lo + n_rows)
            run = jnp.zeros((n_rows, 1), F32) if first else run_ref[hd, rows, :]
            sp16 = softplus[i].astype(BF16)
            blocks = _cumsum_blocks(width)
            parts = [None] * len(blocks)
            for idx in reversed(range(len(blocks))):
                b0, bw = blocks[idx]
                parts[idx] = _dot(sp16[:, b0:b0 + bw], u_ref[0:bw, 0:bw]) + run
                run = run - jnp.sum(softplus[i][:, b0:b0 + bw], axis=1, keepdims=True)
            run_ref[hd, rows, :] = run
            later[i] = parts[0] if len(parts) == 1 else jnp.concatenate(parts, axis=1)
        for i, (hd, row_lo, n_rows, kstart, width, masked, first) in enumerate(jobs):
            w[i] = jnp.exp2(log_beta[i] + later[i])
            if masked:
                w[i] = jnp.where(mask[i], w[i], 0.0)
            w[i] = w[i].astype(BF16)
        for i, (hd, row_lo, n_rows, kstart, width, masked, first) in enumerate(jobs):
            rows = slice(row_lo, row_lo + n_rows)
            cols = slice(hd * HEAD_W, (hd + 1) * HEAD_W)
            contrib = _dot(w[i], v_ref[0, pl.ds(kstart, width), cols])
            if first:
                acc_ref[rows, cols] = contrib
            else:
                acc_ref[rows, cols] += contrib

    def before_tile(row_lo, n_rows, kstart, width, masked):
        return [(row_lo + r, BLOCK, kstart, width, masked, False) for r in range(0, n_rows, BLOCK)]

    top = min(SB_TOP_ROWS, tq)
    n_blk = tq // BLOCK

    @pl.when(qi == 0)
    def _():
        attend([(r * BLOCK, BLOCK, q0, (r + 1) * BLOCK, "both", True) for r in range(n_blk)])

    def own(blocks):
        diag = [(r * BLOCK, BLOCK, pl.multiple_of(q0 + r * BLOCK, BLOCK), BLOCK, "causal", True) for r in blocks]
        rest = [(r * BLOCK, BLOCK, q0, r * BLOCK, False, False) for r in blocks if r > 0]
        return diag + rest

    @pl.when(qi > 0)
    def _():
        attend(own(range(top // BLOCK)))
        attend(own(range(top // BLOCK, n_blk))
               + before_tile(0, top, pl.multiple_of(q0 - SB_KEY_CHUNK, BLOCK), SB_KEY_CHUNK, False))

    def walk(row_lo, n_rows, chunks_done):
        def more_needed():
            return jnp.max(run_ref[:, row_lo:row_lo + n_rows, :]) > -ZERO_PROB_LOG2

        def chunk(kstart, width, masked):
            attend(before_tile(row_lo, n_rows, pl.multiple_of(kstart, BLOCK), width, masked))

        def cond(carry):
            kstart, go = carry
            return (kstart >= BLOCK) & go

        def body(carry):
            kstart, _ = carry
            chunk(kstart, SB_KEY_CHUNK, False)
            return kstart - SB_KEY_CHUNK, more_needed()

        kstart, go = lax.while_loop(cond, body, (q0 - (chunks_done + 1) * SB_KEY_CHUNK, more_needed()))

        def cond_tail(carry):
            kstart, go = carry
            return (kstart >= 0) & go

        def body_tail(carry):
            kstart, _ = carry
            chunk(kstart, BLOCK, "pad")
            return kstart - BLOCK, more_needed()

        lax.while_loop(cond_tail, body_tail, (kstart + SB_KEY_CHUNK - BLOCK, go))

    walk(0, top, 1)
    if top < tq:
        @pl.when(jnp.max(run_ref[:, top:, :]) > -ZERO_PROB_LOG2)
        def _():
            walk(top, tq - top, 0)

    o_ref[0] = acc_ref[...].astype(BF16)


def _sb_attention(qkv):
    b, p, _ = qkv.shape
    nh = N_SB_HEADS
    tq = _pick_tile(p, (640, 512, 256, 128))
    umax = SB_KEY_CHUNK
    jj = np.arange(umax)
    u = jnp.asarray(-(jj[:, None] > jj[None, :]).astype(np.float32), BF16)
    heads = SB_HEADS_PER_STEP
    groups = nh // heads
    gw = heads * HEAD_W
    kv_block = (1, p, gw)
    return pl.pallas_call(
        functools.partial(_sb_attn_kernel, tq=tq, heads=heads),
        grid=(b, groups, p // tq),
        in_specs=[pl.BlockSpec((1, tq, gw), lambda bi, g, i: (bi, i, g)),
                  pl.BlockSpec(kv_block, lambda bi, g, i: (bi, 0, groups + g)),
                  pl.BlockSpec(kv_block, lambda bi, g, i: (bi, 0, 2 * groups + g)),
                  _const_spec((umax, umax))],
        out_specs=pl.BlockSpec((1, tq, gw), lambda bi, g, i: (bi, i, g)),
        out_shape=jax.ShapeDtypeStruct((b, p, nh * HEAD_W), BF16),
        scratch_shapes=[pltpu.VMEM((heads, tq, 1), F32), pltpu.VMEM((tq, gw), F32)],
        compiler_params=_params(("parallel", "parallel", "arbitrary")),
    )(qkv, qkv, qkv, u)


def kernel(x, meta_tokens, mix_norm, ffn_norm, ffn_up, ffn_conv, ffn_conv_b, ffn_down, ab_w_in, ab_ret_norm,
           ab_diff_norm, ab_lam_q1, ab_lam_k1, ab_lam_q2, ab_lam_k2, ab_w_out, c_w_in, c_w_out, final_norm):
    b, s, d = x.shape
    depth = mix_norm.shape[0]
    p = s + BLOCK
    t = b * p
    h = None
    for i in range(depth):
        if i % 2 == 0:
            e = i // 2
            lambda_init = 0.8 - 0.6 * math.exp(-0.3 * i)
            w_in = ab_w_in[e].astype(BF16)
            scales = (1.0, 1.0, 1.0, 1.0, DIFF_QK_DIM ** -0.5 * LOG2_E, 1.0, 1.0)
            if i == 0:
                proj, h = _embed_norm_proj(x, meta_tokens, mix_norm[i], w_in, scales, 512)
            else:
                proj = _norm_proj(h.reshape(t, d), mix_norm[i], w_in, scales, 512).reshape(b, p, -1)
            ret = _retention(proj, ab_ret_norm[e])
            dif = _diff_attention(proj, ab_lam_q1[e], ab_lam_k1[e], ab_lam_q2[e], ab_lam_k2[e],
                                  ab_diff_norm[e], lambda_init, 4 * N_RET_HEADS)
            acts, w_out = [ret, dif], ab_w_out[e].astype(BF16)
        else:
            o = i // 2
            w_in = c_w_in[o].astype(BF16)
            n_chunks = w_in.shape[1] // 512
            q_chunks = N_SB_HEADS * HEAD_W // 512
            scales = tuple(HEAD_W ** -0.5 * LOG2_E if c < q_chunks else 1.0 for c in range(n_chunks))
            qkv = _norm_proj(h.reshape(t, d), mix_norm[i], w_in, scales, 512).reshape(b, p, -1)
            acts, w_out = [_sb_attention(qkv)], c_w_out[o].astype(BF16)
        h = _mixer_out_ffn(h, acts, w_out, ffn_norm[i], ffn_up[i].astype(BF16), ffn_conv[i], ffn_conv_b[i],
                           ffn_down[i].astype(BF16), final_norm if i == depth - 1 else None)
    return h
```

```python
import functools
import math

import numpy as np
import jax
import jax.numpy as jnp
from jax import lax
from jax.experimental import pallas as pl
from jax.experimental.pallas import tpu as pltpu

F32 = jnp.float32
BF16 = jnp.bfloat16

N_META = 16
BLOCK = 128
N_PAD = BLOCK - N_META
EPS = 1e-6
MASK_VALUE = -1e30
HEAD_W = 128
N_RET_HEADS = 4
N_DIFF_HEADS = 4
DIFF_QK_DIM = 64
N_SB_HEADS = 8
CONV_WIDTH = 3
HALO = 16
FFN_CHUNK = 768
LOG2_E = math.log2(math.e)
INV_LN2 = 1.0 / math.log(2.0)
ZERO_PROB_LOG2 = 160.0
SB_KEY_CHUNK = 256
SB_TOP_ROWS = 256
SB_HEADS_PER_STEP = 2

VMEM_LIMIT_BYTES = 56 * 1024 * 1024


def _pick_tile(n, candidates):
    for c in candidates:
        if n % c == 0:
            return c
    raise ValueError(f"no tile for {n} in {candidates}")


def _params(sem):
    return pltpu.CompilerParams(dimension_semantics=sem, vmem_limit_bytes=VMEM_LIMIT_BYTES)


def _const_spec(shape):
    return pl.BlockSpec(shape, lambda *_: (0,) * len(shape), pipeline_mode=pl.Buffered(1))


def _rms(x, g):
    return x * lax.rsqrt(jnp.mean(x * x, axis=-1, keepdims=True) + EPS) * g


def _dot(a, b):
    return jnp.dot(a, b, preferred_element_type=F32)


def _dot_nt(a, b):
    return lax.dot_general(a, b, (((1,), (1,)), ((), ())), preferred_element_type=F32)


def _dot_tn(a, b):
    return lax.dot_general(a, b, (((0,), (0,)), ((), ())), preferred_element_type=F32)


def _norm_proj_kernel(x_ref, g_ref, w_ref, o_ref, xn_ref, *, chunk, scales):
    xn_ref[...] = _rms(x_ref[...], g_ref[...]).astype(BF16)
    for c, scale in enumerate(scales):
        cols = slice(c * chunk, (c + 1) * chunk)
        r = _dot(xn_ref[...], w_ref[:, cols])
        if scale != 1.0:
            r = r * scale
        o_ref[:, cols] = r.astype(BF16)


def _norm_proj(x2, g, w, scales, chunk):
    t, d = x2.shape
    n = w.shape[1]
    tm = _pick_tile(t, (1024, 640, 512, 256, 128))
    return pl.pallas_call(
        functools.partial(_norm_proj_kernel, chunk=chunk, scales=scales),
        grid=(t // tm,),
        in_specs=[pl.BlockSpec((tm, d), lambda i: (i, 0)),
                  _const_spec((1, d)),
                  _const_spec((d, n))],
        out_specs=pl.BlockSpec((tm, n), lambda i: (i, 0)),
        out_shape=jax.ShapeDtypeStruct((t, n), BF16),
        scratch_shapes=[pltpu.VMEM((tm, d), BF16)],
        compiler_params=_params(("parallel",)),
    )(x2, g.reshape(1, d), w)


def _embed_norm_proj_kernel(x_ref, lead_ref, g_ref, w_ref, o_ref, h_ref, xn_ref, *, chunk, scales):
    tm = h_ref.shape[1]

    @pl.when(pl.program_id(1) == 0)
    def _():
        h_ref[0, 0:BLOCK, :] = lead_ref[...]
        h_ref[0, BLOCK:, :] = x_ref[0, 0:tm - BLOCK, :]

    @pl.when(pl.program_id(1) > 0)
    def _():
        h_ref[0] = x_ref[0]

    xn_ref[...] = _rms(h_ref[0], g_ref[...]).astype(BF16)
    for c, scale in enumerate(scales):
        cols = slice(c * chunk, (c + 1) * chunk)
        r = _dot(xn_ref[...], w_ref[:, cols])
        if scale != 1.0:
            r = r * scale
        o_ref[0, :, cols] = r.astype(BF16)


def _embed_norm_proj(x, meta_tokens, g, w, scales, chunk):
    b, s, d = x.shape
    p = s + BLOCK
    n = w.shape[1]
    tm = _pick_tile(p, (640, 512, 256, 128))
    lead = jnp.concatenate([jnp.zeros((N_PAD, d), x.dtype), meta_tokens.astype(x.dtype)], axis=0)
    x_spec = pl.BlockSpec((pl.Element(1), pl.Element(tm), pl.Element(d)),
                          lambda bi, i: (bi, pl.multiple_of(jnp.maximum(i * tm - BLOCK, 0), BLOCK), 0))
    return pl.pallas_call(
        functools.partial(_embed_norm_proj_kernel, chunk=chunk, scales=scales),
        grid=(b, p // tm),
        in_specs=[x_spec, _const_spec((BLOCK, d)), _const_spec((1, d)), _const_spec((d, n))],
        out_specs=[pl.BlockSpec((1, tm, n), lambda bi, i: (bi, i, 0)),
                   pl.BlockSpec((1, tm, d), lambda bi, i: (bi, i, 0))],
        out_shape=[jax.ShapeDtypeStruct((b, p, n), BF16), jax.ShapeDtypeStruct((b, p, d), F32)],
        scratch_shapes=[pltpu.VMEM((tm, d), BF16)],
        compiler_params=_params(("parallel", "parallel")),
    )(x, lead, g.reshape(1, d), w)


def _ffn_kernel(*refs, tm, f, bounds, final, p_start, act_widths):
    n_act = len(act_widths)
    x_ref, xh_ref = refs[0:2]
    act_refs = refs[2:2 + 2 * n_act]
    pos = 2 + 2 * n_act
    if n_act:
        wo_ref = refs[pos]
        pos += 1
    g_ref, wup_ref, wc_ref, bc_ref, wd_ref = refs[pos:pos + 5]
    pos += 5
    if final:
        fg_ref = refs[pos]
        pos += 1
    o_ref, xn_ref, acc_ref = refs[pos:pos + 3]
    i = pl.program_id(1)
    g = g_ref[...]
    if n_act:
        a_ref = refs[pos + 3]
        col = 0
        for k, width in enumerate(act_widths):
            a_ref[0:HALO, col:col + width] = act_refs[2 * k + 1][0]
            a_ref[HALO:, col:col + width] = act_refs[2 * k][0]
            col += width
        mix = _dot(a_ref[...], wo_ref[...])
        x_halo = xh_ref[0] + mix[0:HALO, :]
        o_ref[0] = x_ref[0] + mix[HALO:, :]
    else:
        x_halo = xh_ref[0]
        o_ref[0] = x_ref[0]
    xn_ref[0:HALO, :] = _rms(x_halo, g).astype(BF16)
    xn_ref[HALO:, :] = _rms(o_ref[0], g).astype(BF16)
    if p_start < N_PAD + HALO:
        masked_rows = HALO + N_PAD - p_start

        @pl.when(i == 0)
        def _():
            xn_ref[0:masked_rows, :] = jnp.zeros((masked_rows, xn_ref.shape[1]), BF16)

    def up_proj(c):
        lo, hi = bounds[c]
        gate = _dot(xn_ref[...], wup_ref[:, lo:hi])
        val = _dot(xn_ref[HALO:, :], wup_ref[:, f + lo:f + hi])
        return gate, val

    n_chunks = len(bounds)
    nxt = up_proj(0)
    for c in range(n_chunks):
        cols = slice(*bounds[c])
        gate, val = nxt
        if c + 1 < n_chunks:
            nxt = up_proj(c + 1)
        conv = bc_ref[:, cols]
        for tap in range(CONV_WIDTH):
            lo = HALO - (CONV_WIDTH - 1) + tap
            conv = conv + gate[lo:lo + tm, :] * wc_ref[tap:tap + 1, cols]
        half = 0.5 * conv
        mid = ((half + half * jnp.tanh(half)) * val).astype(BF16)
        contrib = _dot(mid, wd_ref[cols, :])
        if c == 0:
            acc_ref[...] = contrib
        else:
            acc_ref[...] += contrib
    y = o_ref[0] + acc_ref[...]
    if final:
        y = _rms(y, fg_ref[...])
    o_ref[0] = y


def _mixer_out_ffn(h, acts, w_out, g, w_up, w_conv, b_conv, w_down, final_g=None):
    b, p, d = h.shape
    f = w_down.shape[0]
    final = final_g is not None
    p_start = BLOCK if final else 0
    tm = _pick_tile(p - p_start, (1024, 832, 640, 512, 256, 128) if final else (832, 640, 512, 256, 128))
    bounds = tuple((lo, min(lo + FFN_CHUNK, f)) for lo in range(0, f, FFN_CHUNK))
    act_widths = tuple(a.shape[-1] for a in acts)

    def rows_spec(n_rows, back, width):
        return pl.BlockSpec(
            (pl.Element(1), pl.Element(n_rows), pl.Element(width)),
            lambda bi, i: (bi, pl.multiple_of(jnp.maximum(p_start + i * tm - back, 0), HALO), 0))

    in_specs = [rows_spec(tm, 0, d), rows_spec(HALO, HALO, d)]
    args = [h, h]
    for a, width in zip(acts, act_widths):
        in_specs += [rows_spec(tm, 0, width), rows_spec(HALO, HALO, width)]
        args += [a, a]
    in_specs.append(_const_spec(w_out.shape))
    args.append(w_out)
    in_specs += [_const_spec((1, d)),
                 _const_spec((d, 2 * f)),
                 _const_spec((CONV_WIDTH, f)),
                 _const_spec((1, f)),
                 _const_spec((f, d))]
    args += [g.reshape(1, d), w_up, w_conv, b_conv.reshape(1, f), w_down]
    if final:
        in_specs.append(_const_spec((1, d)))
        args.append(final_g.reshape(1, d))
    return pl.pallas_call(
        functools.partial(_ffn_kernel, tm=tm, f=f, bounds=bounds, final=final, p_start=p_start,
                          act_widths=act_widths),
        grid=(b, (p - p_start) // tm),
        in_specs=in_specs,
        out_specs=pl.BlockSpec((1, tm, d), lambda bi, i: (bi, i, 0)),
        out_shape=jax.ShapeDtypeStruct((b, p - p_start, d), F32),
        scratch_shapes=[pltpu.VMEM((tm + HALO, d), BF16), pltpu.VMEM((tm, d), F32),
                        pltpu.VMEM((tm + HALO, sum(act_widths)), BF16)],
        compiler_params=_params(("parallel", "parallel")),
    )(*args)


def _retention_constants():
    c = BLOCK
    hh = np.arange(N_RET_HEADS, dtype=np.float64)
    log_g = np.log1p(-(2.0 ** (-5.0 - hh)))
    j = np.arange(c, dtype=np.float64)
    rel = j[:, None] - j[None, :]
    scale = HEAD_W ** -0.5
    decay = np.where(rel >= 0, np.exp(log_g[:, None, None] * np.maximum(rel, 0.0)), 0.0) * scale
    q_decay = np.broadcast_to(np.exp(log_g[:, None] * (j + 1.0))[:, :, None], (N_RET_HEADS, c, HEAD_W))
    k_decay = np.broadcast_to((np.exp(log_g[:, None] * (c - 1.0 - j)) * scale)[:, :, None],
                              (N_RET_HEADS, c, HEAD_W))
    chunk_decay = tuple(float(v) for v in np.exp(log_g * c))
    return (jnp.asarray(decay, F32), jnp.asarray(q_decay, F32), jnp.asarray(k_decay, F32), chunk_decay)


def _retention_kernel(q_ref, k_ref, v_ref, gate_ref, dmat_ref, qd_ref, kd_ref, gn_ref, o_ref, state_ref,
                      *, tr, chunk_decay):
    i = pl.program_id(1)

    @pl.when(i == 0)
    def _():
        state_ref[...] = jnp.zeros_like(state_ref)

    for c in range(tr // BLOCK):
        rows = slice(c * BLOCK, (c + 1) * BLOCK)
        if c == 0:
            pos = i * tr + lax.broadcasted_iota(jnp.int32, (BLOCK, HEAD_W), 0)
            valid = pos >= N_PAD
        for h in range(N_RET_HEADS):
            cols = slice(h * HEAD_W, (h + 1) * HEAD_W)
            q = q_ref[0, rows, cols]
            k = k_ref[0, rows, cols]
            v = v_ref[0, rows, cols]
            if c == 0:
                k = jnp.where(valid, k, jnp.zeros_like(k))
                v = jnp.where(valid, v, jnp.zeros_like(v))
            scores = _dot_nt(q, k) * dmat_ref[h]
            inner = _dot(scores.astype(BF16), v)
            state = state_ref[h]
            cross = _dot(q, state.astype(BF16)) * qd_ref[h]
            k_scaled = (k.astype(F32) * kd_ref[h]).astype(BF16)
            state_ref[h] = chunk_decay[h] * state + _dot_tn(k_scaled, v)
            o = inner + cross
            oc = o - jnp.mean(o, axis=-1, keepdims=True)
            y = oc * lax.rsqrt(jnp.mean(oc * oc, axis=-1, keepdims=True) + EPS) * gn_ref[:, cols]
            gate = gate_ref[0, rows, cols].astype(F32)
            o_ref[0, rows, cols] = (y * (gate / (1.0 + jnp.exp(-gate)))).astype(BF16)


def _retention(proj, gn):
    b, p, _ = proj.shape
    w = N_RET_HEADS * HEAD_W
    tr = _pick_tile(p, (640, 512, 256, 128))
    dmat, qd, kd, chunk_decay = _retention_constants()

    def col_spec(cb):
        return pl.BlockSpec((1, tr, w), lambda bi, i: (bi, i, cb))

    cshape = (N_RET_HEADS, BLOCK, HEAD_W)
    return pl.pallas_call(
        functools.partial(_retention_kernel, tr=tr, chunk_decay=chunk_decay),
        grid=(b, p // tr),
        in_specs=[col_spec(0), col_spec(1), col_spec(2), col_spec(3),
                  _const_spec(cshape), _const_spec(cshape), _const_spec(cshape), _const_spec((1, w))],
        out_specs=pl.BlockSpec((1, tr, w), lambda bi, i: (bi, i, 0)),
        out_shape=jax.ShapeDtypeStruct((b, p, w), BF16),
        scratch_shapes=[pltpu.VMEM(cshape, F32)],
        compiler_params=_params(("parallel", "arbitrary")),
    )(proj, proj, proj, proj, dmat, qd, kd, gn.reshape(1, w))


def _diff_attn_kernel(slope_ref, lq1_ref, lk1_ref, lq2_ref, lk2_ref, q_ref, k_ref, v_ref, sel_ref, gn_ref, o_ref,
                      qs_ref, m_ref, l_ref, acc_ref, kmax_ref, s0_ref, s1_ref, slack_ref, *, tq, tk, rsub, lambda_init):
    h = pl.program_id(1)
    qi = pl.program_id(2)
    slope = slope_ref[h] * LOG2_E
    q0 = qi * tq
    n_sub = 2 * tq // rsub

    @pl.when(qi == 0)
    def _():
        kf = k_ref[0].astype(F32)
        norms = _dot((kf * kf).astype(BF16), sel_ref[...])
        kmax_ref[...] = jnp.sqrt(jnp.max(norms, axis=0, keepdims=True)) * 1.02

    q = q_ref[0]
    lane = lax.broadcasted_iota(jnp.int32, (tq, HEAD_W), 1)
    qs_ref[0:tq, :] = jnp.where(lane < DIFF_QK_DIM, q, jnp.zeros_like(q))
    qs_ref[tq:, :] = jnp.where(lane >= DIFF_QK_DIM, q, jnp.zeros_like(q))
    m_ref[...] = jnp.full_like(m_ref, MASK_VALUE)
    l_ref[...] = jnp.zeros_like(l_ref)
    acc_ref[...] = jnp.zeros_like(acc_ref)

    def tile_start(j):
        return j * tk if isinstance(j, int) else pl.multiple_of(j * tk, tk)

    s_refs = (s0_ref, s1_ref)

    def issue_scores(j, slot, per_block=False):
        k = k_ref[0, pl.ds(tile_start(j), tk), :]
        if per_block:
            for r in range(n_sub):
                rows = slice(r * rsub, (r + 1) * rsub)
                s_refs[slot][rows, :] = _dot_nt(qs_ref[rows, :], k)
        else:
            s_refs[slot][...] = _dot_nt(qs_ref[...], k)

    def consume(j, slot, mode, prefetch):
        s_ref = s_refs[slot]
        if mode == "diag":
            issue_scores(j, slot, per_block=True)
        if prefetch is not None:
            issue_scores(prefetch, 1 - slot)
        start = tile_start(j)
        kpos = j * tk + lax.broadcasted_iota(jnp.int32, (1, tk), 1)
        bias = slope * (kpos - q0).astype(F32)
        v = v_ref[0, pl.ds(start, tk), :]

        def stage_b(r):
            rows = slice(r * rsub, (r + 1) * rsub)
            s = s_ref[rows, :] + bias
            if mode == "first":
                s = jnp.where(kpos >= N_PAD, s, MASK_VALUE)
            elif mode == "diag":
                qpos = q0 + (r * rsub) % tq + lax.broadcasted_iota(jnp.int32, (rsub, 1), 0)
                s = jnp.where((qpos >= kpos) & (kpos >= N_PAD), s, MASK_VALUE)
            m_prev = m_ref[rows, :]
            m_new = jnp.maximum(m_prev, jnp.max(s, axis=1, keepdims=True))
            p = jnp.exp2(s - m_new)
            alpha = jnp.exp2(m_prev - m_new)
            l_ref[rows, :] = alpha * l_ref[rows, :] + jnp.sum(p, axis=1, keepdims=True)
            m_ref[rows, :] = m_new
            return p.astype(BF16), alpha

        def stage_c(r, p, alpha):
            rows = slice(r * rsub, (r + 1) * rsub)
            acc_ref[rows, :] = alpha * acc_ref[rows, :] + _dot(p, v)

        b_out = {}
        for t in range(n_sub + 1):
            if t < n_sub:
                b_out[t] = stage_b(t)
            if t >= 1:
                stage_c(t - 1, *b_out.pop(t - 1))

    @pl.when(qi >= 0)
    def _():
        qf = qs_ref[...].astype(F32)
        qnorm = jnp.sqrt(jnp.sum(qf * qf, axis=1, keepdims=True))
        row = lax.broadcasted_iota(jnp.int32, (2 * tq, 1), 0)
        bound = qnorm * jnp.where(row < tq, kmax_ref[:, 0:1], kmax_ref[:, 1:2])
        consume(qi, 0, "diag", jnp.maximum(qi - 1, 0))
        slack_ref[...] = jnp.max(bound - m_ref[...], axis=0, keepdims=True)

    slack = jnp.max(slack_ref[...])

    def needed(rel):
        return slack + slope * rel >= -ZERO_PROB_LOG2

    def cond(carry):
        j, rel, _ = carry
        return (j >= 1) & needed(rel)

    def body(carry):
        j, rel, slot = carry
        for static_slot in (0, 1):
            @pl.when(slot == static_slot)
            def _():
                consume(j, static_slot, "plain", j - 1)
        return j - 1, rel - tk, 1 - slot

    j_end, rel_end, slot_end = lax.while_loop(cond, body, (qi - 1, jnp.float32(-1.0), jnp.int32(1)))

    for static_slot in (0, 1):
        @pl.when((j_end == 0) & needed(rel_end) & (slot_end == static_slot))
        def _():
            consume(0, static_slot, "first", None)

    o = acc_ref[...] / l_ref[...]
    lam = (jnp.exp(jnp.sum(lq1_ref[...] * lk1_ref[...], axis=1, keepdims=True))
           - jnp.exp(jnp.sum(lq2_ref[...] * lk2_ref[...], axis=1, keepdims=True)) + lambda_init)
    o = o[0:tq, :] - lam * o[tq:, :]
    o_ref[0] = (_rms(o, gn_ref[...]) * (1.0 - lambda_init)).astype(BF16)


def _diff_attention(proj, lam_q1, lam_k1, lam_q2, lam_k2, gn, lambda_init, col0):
    b, p, _ = proj.shape
    nh = N_DIFF_HEADS
    tq = tk = _pick_tile(p, (640, 512, 256, 128))
    rsub = tq
    slopes = jnp.asarray(2.0 ** (-8.0 * (np.arange(nh) + 1.0) / nh), F32)
    lam_spec = _const_spec((1, DIFF_QK_DIM))
    kv_block = (1, p, HEAD_W)
    sel = np.zeros((HEAD_W, HEAD_W), np.float32)
    sel[:DIFF_QK_DIM, 0] = 1.0
    sel[DIFF_QK_DIM:, 1] = 1.0
    return pl.pallas_call(
        functools.partial(_diff_attn_kernel, tq=tq, tk=tk, rsub=rsub, lambda_init=lambda_init),
        grid=(b, nh, p // tq),
        in_specs=[pl.BlockSpec(memory_space=pltpu.SMEM),
                  lam_spec, lam_spec, lam_spec, lam_spec,
                  pl.BlockSpec((1, tq, HEAD_W), lambda bi, h, i: (bi, i, col0 + h)),
                  pl.BlockSpec(kv_block, lambda bi, h, i: (bi, 0, col0 + nh + h)),
                  pl.BlockSpec(kv_block, lambda bi, h, i: (bi, 0, col0 + 2 * nh + h)),
                  _const_spec((HEAD_W, HEAD_W)),
                  pl.BlockSpec((1, HEAD_W), lambda bi, h, i: (0, h))],
        out_specs=pl.BlockSpec((1, tq, HEAD_W), lambda bi, h, i: (bi, i, h)),
        out_shape=jax.ShapeDtypeStruct((b, p, nh * HEAD_W), BF16),
        scratch_shapes=[pltpu.VMEM((2 * tq, HEAD_W), BF16),
                        pltpu.VMEM((2 * tq, 1), F32),
                        pltpu.VMEM((2 * tq, 1), F32),
                        pltpu.VMEM((2 * tq, HEAD_W), F32),
                        pltpu.VMEM((1, HEAD_W), F32),
                        pltpu.VMEM((2 * tq, tk), F32),
                        pltpu.VMEM((2 * tq, tk), F32),
                        pltpu.VMEM((1, 1), F32)],
        compiler_params=_params(("parallel", "parallel", "arbitrary")),
    )(slopes, lam_q1.reshape(1, -1), lam_k1.reshape(1, -1), lam_q2.reshape(1, -1), lam_k2.reshape(1, -1),
      proj, proj, proj, jnp.asarray(sel, BF16), gn.reshape(1, -1))


def _cumsum_blocks(tk):
    blocks, start = [], 0
    while start < tk:
        width = 256 if tk - start >= 256 else tk - start
        blocks.append((start, width))
        start += width
    return tuple(blocks)


def _sb_attn_kernel(q_ref, k_ref, v_ref, u_ref, o_ref, run_ref, acc_ref, *, tq, heads):
    qi = pl.program_id(2)
    q0 = pl.multiple_of(qi * tq, BLOCK)

    def attend(pieces):
        jobs = [(hd,) + tuple(piece) for piece in pieces for hd in range(heads)]
        n = len(jobs)
        z, softplus, log_beta, mask, later, w = ([None] * n for _ in range(6))
        for i, (hd, row_lo, n_rows, kstart, width, masked, first) in enumerate(jobs):
            cols = slice(hd * HEAD_W, (hd + 1) * HEAD_W)
            z[i] = _dot_nt(q_ref[0, row_lo:row_lo + n_rows, cols], k_ref[0, pl.ds(kstart, width), cols])
        for i, (hd, row_lo, n_rows, kstart, width, masked, first) in enumerate(jobs):
            neg_abs = lax.bitcast_convert_type(
                lax.bitcast_convert_type(z[i], jnp.uint32) | jnp.uint32(0x80000000), F32)
            sp = jnp.maximum(z[i], 0.0) + jnp.log(1.0 + jnp.exp2(neg_abs)) * INV_LN2
            log_beta[i] = z[i] - sp
            if masked:
                kpos = kstart + lax.broadcasted_iota(jnp.int32, (1, width), 1)
                if masked == "pad":
                    mask[i] = jnp.broadcast_to(kpos >= N_PAD, sp.shape)
                else:
                    qpos = q0 + row_lo + lax.broadcasted_iota(jnp.int32, (n_rows, 1), 0)
                    mask[i] = (qpos > kpos) if masked == "causal" else (qpos > kpos) & (kpos >= N_PAD)
                sp = jnp.where(mask[i], sp, 0.0)
            softplus[i] = sp
        for i, (hd, row_lo, n_rows, kstart, width, masked, first) in enumerate(jobs):
            rows = slice(row_lo, row_lo + n_rows)
            run = jnp.zeros((n_rows, 1), F32) if first else run_ref[hd, rows, :]
            sp16 = softplus[i].astype(BF16)
            blocks = _cumsum_blocks(width)
            parts = [None] * len(blocks)
            for idx in reversed(range(len(blocks))):
                b0, bw = blocks[idx]
                parts[idx] = _dot(sp16[:, b0:b0 + bw], u_ref[0:bw, 0:bw]) + run
                run = run - jnp.sum(softplus[i][:, b0:b0 + bw], axis=1, keepdims=True)
            run_ref[hd, rows, :] = run
            later[i] = parts[0] if len(parts) == 1 else jnp.concatenate(parts, axis=1)
        for i, (hd, row_lo, n_rows, kstart, width, masked, first) in enumerate(jobs):
            w[i] = jnp.exp2(log_beta[i] + later[i])
            if masked:
                w[i] = jnp.where(mask[i], w[i], 0.0)
            w[i] = w[i].astype(BF16)
        for i, (hd, row_lo, n_rows, kstart, width, masked, first) in enumerate(jobs):
            rows = slice(row_lo, row_lo + n_rows)
            cols = slice(hd * HEAD_W, (hd + 1) * HEAD_W)
            contrib = _dot(w[i], v_ref[0, pl.ds(kstart, width), cols])
            if first:
                acc_ref[rows, cols] = contrib
            else:
                acc_ref[rows, cols] += contrib

    def before_tile(row_lo, n_rows, kstart, width, masked):
        return [(row_lo + r, BLOCK, kstart, width, masked, False) for r in range(0, n_rows, BLOCK)]

    top = min(SB_TOP_ROWS, tq)
    n_blk = tq // BLOCK

    @pl.when(qi == 0)
    def _():
        attend([(r * BLOCK, BLOCK, q0, (r + 1) * BLOCK, "both", True) for r in range(n_blk)])

    def own(blocks):
        diag = [(r * BLOCK, BLOCK, pl.multiple_of(q0 + r * BLOCK, BLOCK), BLOCK, "causal", True) for r in blocks]
        rest = [(r * BLOCK, BLOCK, q0, r * BLOCK, False, False) for r in blocks if r > 0]
        return diag + rest

    @pl.when(qi > 0)
    def _():
        attend(own(range(top // BLOCK)))
        attend(own(range(top // BLOCK, n_blk))
               + before_tile(0, top, pl.multiple_of(q0 - SB_KEY_CHUNK, BLOCK), SB_KEY_CHUNK, False))

    def walk(row_lo, n_rows, chunks_done):
        def more_needed():
            return jnp.max(run_ref[:, row_lo:row_lo + n_rows, :]) > -ZERO_PROB_LOG2

        def chunk(kstart, width, masked):
            attend(before_tile(row_lo, n_rows, pl.multiple_of(kstart, BLOCK), width, masked))

        def cond(carry):
            kstart, go = carry
            return (kstart >= BLOCK) & go

        def body(carry):
            kstart, _ = carry
            chunk(kstart, SB_KEY_CHUNK, False)
            return kstart - SB_KEY_CHUNK, more_needed()

        kstart, go = lax.while_loop(cond, body, (q0 - (chunks_done + 1) * SB_KEY_CHUNK, more_needed()))

        def cond_tail(carry):
            kstart, go = carry
            return (kstart >= 0) & go

        def body_tail(carry):
            kstart, _ = carry
            chunk(kstart, BLOCK, "pad")
            return kstart - BLOCK, more_needed()

        lax.while_loop(cond_tail, body_tail, (kstart + SB_KEY_CHUNK - BLOCK, go))

    walk(0, top, 1)
    if top < tq:
        @pl.when(jnp.max(run_ref[:, top:, :]) > -ZERO_PROB_LOG2)
        def _():
            walk(top, tq - top, 0)

    o_ref[0] = acc_ref[...].astype(BF16)


def _sb_attention(qkv):
    b, p, _ = qkv.shape
    nh = N_SB_HEADS
    tq = _pick_tile(p, (640, 512, 256, 128))
    umax = SB_KEY_CHUNK
    jj = np.arange(umax)
    u = jnp.asarray(-(jj[:, None] > jj[None, :]).astype(np.float32), BF16)
    heads = SB_HEADS_PER_STEP
    groups = nh // heads
    gw = heads * HEAD_W
    kv_block = (1, p, gw)
    return pl.pallas_call(
        functools.partial(_sb_attn_kernel, tq=tq, heads=heads),
        grid=(b, groups, p // tq),
        in_specs=[pl.BlockSpec((1, tq, gw), lambda bi, g, i: (bi, i, g)),
                  pl.BlockSpec(kv_block, lambda bi, g, i: (bi, 0, groups + g)),
                  pl.BlockSpec(kv_block, lambda bi, g, i: (bi, 0, 2 * groups + g)),
                  _const_spec((umax, umax))],
        out_specs=pl.BlockSpec((1, tq, gw), lambda bi, g, i: (bi, i, g)),
        out_shape=jax.ShapeDtypeStruct((b, p, nh * HEAD_W), BF16),
        scratch_shapes=[pltpu.VMEM((heads, tq, 1), F32), pltpu.VMEM((tq, gw), F32)],
        compiler_params=_params(("parallel", "parallel", "arbitrary")),
    )(qkv, qkv, qkv, u)


def kernel(x, meta_tokens, mix_norm, ffn_norm, ffn_up, ffn_conv, ffn_conv_b, ffn_down, ab_w_in, ab_ret_norm,
           ab_diff_norm, ab_lam_q1, ab_lam_k1, ab_lam_q2, ab_lam_k2, ab_w_out, c_w_in, c_w_out, final_norm):
    b, s, d = x.shape
    depth = mix_norm.shape[0]
    p = s + BLOCK
    t = b * p
    h = None
    for i in range(depth):
        if i % 2 == 0:
            e = i // 2
            lambda_init = 0.8 - 0.6 * math.exp(-0.3 * i)
            w_in = ab_w_in[e].astype(BF16)
            scales = (1.0, 1.0, 1.0, 1.0, DIFF_QK_DIM ** -0.5 * LOG2_E, 1.0, 1.0)
            if i == 0:
                proj, h = _embed_norm_proj(x, meta_tokens, mix_norm[i], w_in, scales, 512)
            else:
                proj = _norm_proj(h.reshape(t, d), mix_norm[i], w_in, scales, 512).reshape(b, p, -1)
            ret = _retention(proj, ab_ret_norm[e])
            dif = _diff_attention(proj, ab_lam_q1[e], ab_lam_k1[e], ab_lam_q2[e], ab_lam_k2[e],
                                  ab_diff_norm[e], lambda_init, 4 * N_RET_HEADS)
            acts, w_out = [ret, dif], ab_w_out[e].astype(BF16)
        else:
            o = i // 2
            w_in = c_w_in[o].astype(BF16)
            n_chunks = w_in.shape[1] // 512
            q_chunks = N_SB_HEADS * HEAD_W // 512
            scales = tuple(HEAD_W ** -0.5 * LOG2_E if c < q_chunks else 1.0 for c in range(n_chunks))
            qkv = _norm_proj(h.reshape(t, d), mix_norm[i], w_in, scales, 512).reshape(b, p, -1)
            acts, w_out = [_sb_attention(qkv)], c_w_out[o].astype(BF16)
        h = _mixer_out_ffn(h, acts, w_out, ffn_norm[i], ffn_up[i].astype(BF16), ffn_conv[i], ffn_conv_b[i],
                           ffn_down[i].astype(BF16), final_norm if i == depth - 1 else None)
    return h
```

```python
import functools
import math

import numpy as np
import jax
import jax.numpy as jnp
from jax import lax
from jax.experimental import pallas as pl
from jax.experimental.pallas import tpu as pltpu

F32 = jnp.float32
BF16 = jnp.bfloat16

N_META = 16
BLOCK = 128
N_PAD = BLOCK - N_META
EPS = 1e-6
MASK_VALUE = -1e30
HEAD_W = 128
N_RET_HEADS = 4
N_DIFF_HEADS = 4
DIFF_QK_DIM = 64
N_SB_HEADS = 8
CONV_WIDTH = 3
HALO = 16
FFN_CHUNK = 768
LOG2_E = math.log2(math.e)
INV_LN2 = 1.0 / math.log(2.0)
ZERO_PROB_LOG2 = 160.0
SB_KEY_CHUNK = 256
SB_TOP_ROWS = 256
SB_HEADS_PER_STEP = 4

VMEM_LIMIT_BYTES = 56 * 1024 * 1024


def _pick_tile(n, candidates):
    for c in candidates:
        if n % c == 0:
            return c
    raise ValueError(f"no tile for {n} in {candidates}")


def _params(sem):
    return pltpu.CompilerParams(dimension_semantics=sem, vmem_limit_bytes=VMEM_LIMIT_BYTES)


def _const_spec(shape):
    return pl.BlockSpec(shape, lambda *_: (0,) * len(shape), pipeline_mode=pl.Buffered(1))


def _rms(x, g):
    return x * lax.rsqrt(jnp.mean(x * x, axis=-1, keepdims=True) + EPS) * g


def _dot(a, b):
    return jnp.dot(a, b, preferred_element_type=F32)


def _dot_nt(a, b):
    return lax.dot_general(a, b, (((1,), (1,)), ((), ())), preferred_element_type=F32)


def _dot_tn(a, b):
    return lax.dot_general(a, b, (((0,), (0,)), ((), ())), preferred_element_type=F32)


def _norm_proj_kernel(x_ref, g_ref, w_ref, o_ref, xn_ref, *, chunk, scales):
    xn_ref[...] = _rms(x_ref[...], g_ref[...]).astype(BF16)
    for c, scale in enumerate(scales):
        cols = slice(c * chunk, (c + 1) * chunk)
        r = _dot(xn_ref[...], w_ref[:, cols])
        if scale != 1.0:
            r = r * scale
        o_ref[:, cols] = r.astype(BF16)


def _norm_proj(x2, g, w, scales, chunk):
    t, d = x2.shape
    n = w.shape[1]
    tm = _pick_tile(t, (1024, 640, 512, 256, 128))
    return pl.pallas_call(
        functools.partial(_norm_proj_kernel, chunk=chunk, scales=scales),
        grid=(t // tm,),
        in_specs=[pl.BlockSpec((tm, d), lambda i: (i, 0)),
                  _const_spec((1, d)),
                  _const_spec((d, n))],
        out_specs=pl.BlockSpec((tm, n), lambda i: (i, 0)),
        out_shape=jax.ShapeDtypeStruct((t, n), BF16),
        scratch_shapes=[pltpu.VMEM((tm, d), BF16)],
        compiler_params=_params(("parallel",)),
    )(x2, g.reshape(1, d), w)


def _embed_norm_proj_kernel(x_ref, lead_ref, g_ref, w_ref, o_ref, h_ref, xn_ref, *, chunk, scales):
    tm = h_ref.shape[1]

    @pl.when(pl.program_id(1) == 0)
    def _():
        h_ref[0, 0:BLOCK, :] = lead_ref[...]
        h_ref[0, BLOCK:, :] = x_ref[0, 0:tm - BLOCK, :]

    @pl.when(pl.program_id(1) > 0)
    def _():
        h_ref[0] = x_ref[0]

    xn_ref[...] = _rms(h_ref[0], g_ref[...]).astype(BF16)
    for c, scale in enumerate(scales):
        cols = slice(c * chunk, (c + 1) * chunk)
        r = _dot(xn_ref[...], w_ref[:, cols])
        if scale != 1.0:
            r = r * scale
        o_ref[0, :, cols] = r.astype(BF16)


def _embed_norm_proj(x, meta_tokens, g, w, scales, chunk):
    b, s, d = x.shape
    p = s + BLOCK
    n = w.shape[1]
    tm = _pick_tile(p, (640, 512, 256, 128))
    lead = jnp.concatenate([jnp.zeros((N_PAD, d), x.dtype), meta_tokens.astype(x.dtype)], axis=0)
    x_spec = pl.BlockSpec((pl.Element(1), pl.Element(tm), pl.Element(d)),
                          lambda bi, i: (bi, pl.multiple_of(jnp.maximum(i * tm - BLOCK, 0), BLOCK), 0))
    return pl.pallas_call(
        functools.partial(_embed_norm_proj_kernel, chunk=chunk, scales=scales),
        grid=(b, p // tm),
        in_specs=[x_spec, _const_spec((BLOCK, d)), _const_spec((1, d)), _const_spec((d, n))],
        out_specs=[pl.BlockSpec((1, tm, n), lambda bi, i: (bi, i, 0)),
                   pl.BlockSpec((1, tm, d), lambda bi, i: (bi, i, 0))],
        out_shape=[jax.ShapeDtypeStruct((b, p, n), BF16), jax.ShapeDtypeStruct((b, p, d), F32)],
        scratch_shapes=[pltpu.VMEM((tm, d), BF16)],
        compiler_params=_params(("parallel", "parallel")),
    )(x, lead, g.reshape(1, d), w)


def _ffn_kernel(*refs, tm, f, bounds, final, p_start, act_widths):
    n_act = len(act_widths)
    x_ref, xh_ref = refs[0:2]
    act_refs = refs[2:2 + 2 * n_act]
    pos = 2 + 2 * n_act
    if n_act:
        wo_ref = refs[pos]
        pos += 1
    g_ref, wup_ref, wc_ref, bc_ref, wd_ref = refs[pos:pos + 5]
    pos += 5
    if final:
        fg_ref = refs[pos]
        pos += 1
    o_ref, xn_ref, acc_ref = refs[pos:pos + 3]
    i = pl.program_id(1)
    g = g_ref[...]
    if n_act:
        a_ref = refs[pos + 3]
        col = 0
        for k, width in enumerate(act_widths):
            a_ref[0:HALO, col:col + width] = act_refs[2 * k + 1][0]
            a_ref[HALO:, col:col + width] = act_refs[2 * k][0]
            col += width
        mix = _dot(a_ref[...], wo_ref[...])
        x_halo = xh_ref[0] + mix[0:HALO, :]
        o_ref[0] = x_ref[0] + mix[HALO:, :]
    else:
        x_halo = xh_ref[0]
        o_ref[0] = x_ref[0]
    xn_ref[0:HALO, :] = _rms(x_halo, g).astype(BF16)
    xn_ref[HALO:, :] = _rms(o_ref[0], g).astype(BF16)
    if p_start < N_PAD + HALO:
        masked_rows = HALO + N_PAD - p_start

        @pl.when(i == 0)
        def _():
            xn_ref[0:masked_rows, :] = jnp.zeros((masked_rows, xn_ref.shape[1]), BF16)

    def up_proj(c):
        lo, hi = bounds[c]
        gate = _dot(xn_ref[...], wup_ref[:, lo:hi])
        val = _dot(xn_ref[HALO:, :], wup_ref[:, f + lo:f + hi])
        return gate, val

    n_chunks = len(bounds)
    nxt = up_proj(0)
    for c in range(n_chunks):
        cols = slice(*bounds[c])
        gate, val = nxt
        if c + 1 < n_chunks:
            nxt = up_proj(c + 1)
        conv = bc_ref[:, cols]
        for tap in range(CONV_WIDTH):
            lo = HALO - (CONV_WIDTH - 1) + tap
            conv = conv + gate[lo:lo + tm, :] * wc_ref[tap:tap + 1, cols]
        half = 0.5 * conv
        mid = ((half + half * jnp.tanh(half)) * val).astype(BF16)
        contrib = _dot(mid, wd_ref[cols, :])
        if c == 0:
            acc_ref[...] = contrib
        else:
            acc_ref[...] += contrib
    y = o_ref[0] + acc_ref[...]
    if final:
        y = _rms(y, fg_ref[...])
    o_ref[0] = y


def _mixer_out_ffn(h, acts, w_out, g, w_up, w_conv, b_conv, w_down, final_g=None):
    b, p, d = h.shape
    f = w_down.shape[0]
    final = final_g is not None
    p_start = BLOCK if final else 0
    tm = _pick_tile(p - p_start, (1024, 832, 640, 512, 256, 128) if final else (832, 640, 512, 256, 128))
    bounds = tuple((lo, min(lo + FFN_CHUNK, f)) for lo in range(0, f, FFN_CHUNK))
    act_widths = tuple(a.shape[-1] for a in acts)

    def rows_spec(n_rows, back, width):
        return pl.BlockSpec(
            (pl.Element(1), pl.Element(n_rows), pl.Element(width)),
            lambda bi, i: (bi, pl.multiple_of(jnp.maximum(p_start + i * tm - back, 0), HALO), 0))

    in_specs = [rows_spec(tm, 0, d), rows_spec(HALO, HALO, d)]
    args = [h, h]
    for a, width in zip(acts, act_widths):
        in_specs += [rows_spec(tm, 0, width), rows_spec(HALO, HALO, width)]
        args += [a, a]
    in_specs.append(_const_spec(w_out.shape))
    args.append(w_out)
    in_specs += [_const_spec((1, d)),
                 _const_spec((d, 2 * f)),
                 _const_spec((CONV_WIDTH, f)),
                 _const_spec((1, f)),
                 _const_spec((f, d))]
    args += [g.reshape(1, d), w_up, w_conv, b_conv.reshape(1, f), w_down]
    if final:
        in_specs.append(_const_spec((1, d)))
        args.append(final_g.reshape(1, d))
    return pl.pallas_call(
        functools.partial(_ffn_kernel, tm=tm, f=f, bounds=bounds, final=final, p_start=p_start,
                          act_widths=act_widths),
        grid=(b, (p - p_start) // tm),
        in_specs=in_specs,
        out_specs=pl.BlockSpec((1, tm, d), lambda bi, i: (bi, i, 0)),
        out_shape=jax.ShapeDtypeStruct((b, p - p_start, d), F32),
        scratch_shapes=[pltpu.VMEM((tm + HALO, d), BF16), pltpu.VMEM((tm, d), F32),
                        pltpu.VMEM((tm + HALO, sum(act_widths)), BF16)],
        compiler_params=_params(("parallel", "parallel")),
    )(*args)


def _retention_constants():
    c = BLOCK
    hh = np.arange(N_RET_HEADS, dtype=np.float64)
    log_g = np.log1p(-(2.0 ** (-5.0 - hh)))
    j = np.arange(c, dtype=np.float64)
    rel = j[:, None] - j[None, :]
    scale = HEAD_W ** -0.5
    decay = np.where(rel >= 0, np.exp(log_g[:, None, None] * np.maximum(rel, 0.0)), 0.0) * scale
    q_decay = np.broadcast_to(np.exp(log_g[:, None] * (j + 1.0))[:, :, None], (N_RET_HEADS, c, HEAD_W))
    k_decay = np.broadcast_to((np.exp(log_g[:, None] * (c - 1.0 - j)) * scale)[:, :, None],
                              (N_RET_HEADS, c, HEAD_W))
    chunk_decay = tuple(float(v) for v in np.exp(log_g * c))
    return (jnp.asarray(decay, F32), jnp.asarray(q_decay, F32), jnp.asarray(k_decay, F32), chunk_decay)


def _retention_kernel(q_ref, k_ref, v_ref, gate_ref, dmat_ref, qd_ref, kd_ref, gn_ref, o_ref, state_ref,
                      *, tr, chunk_decay):
    i = pl.program_id(1)

    @pl.when(i == 0)
    def _():
        state_ref[...] = jnp.zeros_like(state_ref)

    for c in range(tr // BLOCK):
        rows = slice(c * BLOCK, (c + 1) * BLOCK)
        if c == 0:
            pos = i * tr + lax.broadcasted_iota(jnp.int32, (BLOCK, HEAD_W), 0)
            valid = pos >= N_PAD
        for h in range(N_RET_HEADS):
            cols = slice(h * HEAD_W, (h + 1) * HEAD_W)
            q = q_ref[0, rows, cols]
            k = k_ref[0, rows, cols]
            v = v_ref[0, rows, cols]
            if c == 0:
                k = jnp.where(valid, k, jnp.zeros_like(k))
                v = jnp.where(valid, v, jnp.zeros_like(v))
            scores = _dot_nt(q, k) * dmat_ref[h]
            inner = _dot(scores.astype(BF16), v)
            state = state_ref[h]
            cross = _dot(q, state.astype(BF16)) * qd_ref[h]
            k_scaled = (k.astype(F32) * kd_ref[h]).astype(BF16)
            state_ref[h] = chunk_decay[h] * state + _dot_tn(k_scaled, v)
            o = inner + cross
            oc = o - jnp.mean(o, axis=-1, keepdims=True)
            y = oc * lax.rsqrt(jnp.mean(oc * oc, axis=-1, keepdims=True) + EPS) * gn_ref[:, cols]
            gate = gate_ref[0, rows, cols].astype(F32)
            o_ref[0, rows, cols] = (y * (gate / (1.0 + jnp.exp(-gate)))).astype(BF16)


def _retention(proj, gn):
    b, p, _ = proj.shape
    w = N_RET_HEADS * HEAD_W
    tr = _pick_tile(p, (640, 512, 256, 128))
    dmat, qd, kd, chunk_decay = _retention_constants()

    def col_spec(cb):
        return pl.BlockSpec((1, tr, w), lambda bi, i: (bi, i, cb))

    cshape = (N_RET_HEADS, BLOCK, HEAD_W)
    return pl.pallas_call(
        functools.partial(_retention_kernel, tr=tr, chunk_decay=chunk_decay),
        grid=(b, p // tr),
        in_specs=[col_spec(0), col_spec(1), col_spec(2), col_spec(3),
                  _const_spec(cshape), _const_spec(cshape), _const_spec(cshape), _const_spec((1, w))],
        out_specs=pl.BlockSpec((1, tr, w), lambda bi, i: (bi, i, 0)),
        out_shape=jax.ShapeDtypeStruct((b, p, w), BF16),
        scratch_shapes=[pltpu.VMEM(cshape, F32)],
        compiler_params=_params(("parallel", "arbitrary")),
    )(proj, proj, proj, proj, dmat, qd, kd, gn.reshape(1, w))


def _diff_attn_kernel(slope_ref, lq1_ref, lk1_ref, lq2_ref, lk2_ref, q_ref, k_ref, v_ref, sel_ref, gn_ref, o_ref,
                      qs_ref, m_ref, l_ref, acc_ref, kmax_ref, s0_ref, s1_ref, slack_ref, *, tq, tk, rsub, lambda_init):
    h = pl.program_id(1)
    qi = pl.program_id(2)
    slope = slope_ref[h] * LOG2_E
    q0 = qi * tq
    n_sub = 2 * tq // rsub

    @pl.when(qi == 0)
    def _():
        kf = k_ref[0].astype(F32)
        norms = _dot((kf * kf).astype(BF16), sel_ref[...])
        kmax_ref[...] = jnp.sqrt(jnp.max(norms, axis=0, keepdims=True)) * 1.02

    q = q_ref[0]
    lane = lax.broadcasted_iota(jnp.int32, (tq, HEAD_W), 1)
    qs_ref[0:tq, :] = jnp.where(lane < DIFF_QK_DIM, q, jnp.zeros_like(q))
    qs_ref[tq:, :] = jnp.where(lane >= DIFF_QK_DIM, q, jnp.zeros_like(q))
    m_ref[...] = jnp.full_like(m_ref, MASK_VALUE)
    l_ref[...] = jnp.zeros_like(l_ref)
    acc_ref[...] = jnp.zeros_like(acc_ref)

    def tile_start(j):
        return j * tk if isinstance(j, int) else pl.multiple_of(j * tk, tk)

    s_refs = (s0_ref, s1_ref)

    def issue_scores(j, slot, per_block=False):
        k = k_ref[0, pl.ds(tile_start(j), tk), :]
        if per_block:
            for r in range(n_sub):
                rows = slice(r * rsub, (r + 1) * rsub)
                s_refs[slot][rows, :] = _dot_nt(qs_ref[rows, :], k)
        else:
            s_refs[slot][...] = _dot_nt(qs_ref[...], k)

    def consume(j, slot, mode, prefetch):
        s_ref = s_refs[slot]
        if mode == "diag":
            issue_scores(j, slot, per_block=True)
        if prefetch is not None:
            issue_scores(prefetch, 1 - slot)
        start = tile_start(j)
        kpos = j * tk + lax.broadcasted_iota(jnp.int32, (1, tk), 1)
        bias = slope * (kpos - q0).astype(F32)
        v = v_ref[0, pl.ds(start, tk), :]

        def stage_b(r):
            rows = slice(r * rsub, (r + 1) * rsub)
            s = s_ref[rows, :] + bias
            if mode == "first":
                s = jnp.where(kpos >= N_PAD, s, MASK_VALUE)
            elif mode == "diag":
                qpos = q0 + (r * rsub) % tq + lax.broadcasted_iota(jnp.int32, (rsub, 1), 0)
                s = jnp.where((qpos >= kpos) & (kpos >= N_PAD), s, MASK_VALUE)
            m_prev = m_ref[rows, :]
            m_new = jnp.maximum(m_prev, jnp.max(s, axis=1, keepdims=True))
            p = jnp.exp2(s - m_new)
            alpha = jnp.exp2(m_prev - m_new)
            l_ref[rows, :] = alpha * l_ref[rows, :] + jnp.sum(p, axis=1, keepdims=True)
            m_ref[rows, :] = m_new
            return p.astype(BF16), alpha

        def stage_c(r, p, alpha):
            rows = slice(r * rsub, (r + 1) * rsub)
            acc_ref[rows, :] = alpha * acc_ref[rows, :] + _dot(p, v)

        b_out = {}
        for t in range(n_sub + 1):
            if t < n_sub:
                b_out[t] = stage_b(t)
            if t >= 1:
                stage_c(t - 1, *b_out.pop(t - 1))

    @pl.when(qi >= 0)
    def _():
        qf = qs_ref[...].astype(F32)
        qnorm = jnp.sqrt(jnp.sum(qf * qf, axis=1, keepdims=True))
        row = lax.broadcasted_iota(jnp.int32, (2 * tq, 1), 0)
        bound = qnorm * jnp.where(row < tq, kmax_ref[:, 0:1], kmax_ref[:, 1:2])
        consume(qi, 0, "diag", jnp.maximum(qi - 1, 0))
        slack_ref[...] = jnp.max(bound - m_ref[...], axis=0, keepdims=True)

    slack = jnp.max(slack_ref[...])

    def needed(rel):
        return slack + slope * rel >= -ZERO_PROB_LOG2

    def cond(carry):
        j, rel = carry
        return (j >= 2) & needed(rel - tk)

    def body(carry):
        j, rel = carry
        consume(j, 1, "plain", j - 1)
        consume(j - 1, 0, "plain", j - 2)
        return j - 2, rel - 2 * tk

    j_mid, rel_mid = lax.while_loop(cond, body, (qi - 1, jnp.float32(-1.0)))
    single = (j_mid >= 1) & needed(rel_mid)

    @pl.when(single)
    def _():
        consume(j_mid, 1, "plain", j_mid - 1)

    j_end = jnp.where(single, j_mid - 1, j_mid)
    rel_end = jnp.where(single, rel_mid - tk, rel_mid)
    for static_slot, in_this_slot in ((0, single), (1, jnp.logical_not(single))):
        @pl.when((j_end == 0) & needed(rel_end) & in_this_slot)
        def _():
            consume(0, static_slot, "first", None)

    o = acc_ref[...] / l_ref[...]
    lam = (jnp.exp(jnp.sum(lq1_ref[...] * lk1_ref[...], axis=1, keepdims=True))
           - jnp.exp(jnp.sum(lq2_ref[...] * lk2_ref[...], axis=1, keepdims=True)) + lambda_init)
    o = o[0:tq, :] - lam * o[tq:, :]
    o_ref[0] = (_rms(o, gn_ref[...]) * (1.0 - lambda_init)).astype(BF16)


def _diff_attention(proj, lam_q1, lam_k1, lam_q2, lam_k2, gn, lambda_init, col0):
    b, p, _ = proj.shape
    nh = N_DIFF_HEADS
    tq = tk = _pick_tile(p, (640, 512, 256, 128))
    rsub = tq
    slopes = jnp.asarray(2.0 ** (-8.0 * (np.arange(nh) + 1.0) / nh), F32)
    lam_spec = _const_spec((1, DIFF_QK_DIM))
    kv_block = (1, p, HEAD_W)
    sel = np.zeros((HEAD_W, HEAD_W), np.float32)
    sel[:DIFF_QK_DIM, 0] = 1.0
    sel[DIFF_QK_DIM:, 1] = 1.0
    return pl.pallas_call(
        functools.partial(_diff_attn_kernel, tq=tq, tk=tk, rsub=rsub, lambda_init=lambda_init),
        grid=(b, nh, p // tq),
        in_specs=[pl.BlockSpec(memory_space=pltpu.SMEM),
                  lam_spec, lam_spec, lam_spec, lam_spec,
                  pl.BlockSpec((1, tq, HEAD_W), lambda bi, h, i: (bi, i, col0 + h)),
                  pl.BlockSpec(kv_block, lambda bi, h, i: (bi, 0, col0 + nh + h)),
                  pl.BlockSpec(kv_block, lambda bi, h, i: (bi, 0, col0 + 2 * nh + h)),
                  _const_spec((HEAD_W, HEAD_W)),
                  pl.BlockSpec((1, HEAD_W), lambda bi, h, i: (0, h))],
        out_specs=pl.BlockSpec((1, tq, HEAD_W), lambda bi, h, i: (bi, i, h)),
        out_shape=jax.ShapeDtypeStruct((b, p, nh * HEAD_W), BF16),
        scratch_shapes=[pltpu.VMEM((2 * tq, HEAD_W), BF16),
                        pltpu.VMEM((2 * tq, 1), F32),
                        pltpu.VMEM((2 * tq, 1), F32),
                        pltpu.VMEM((2 * tq, HEAD_W), F32),
                        pltpu.VMEM((1, HEAD_W), F32),
                        pltpu.VMEM((2 * tq, tk), F32),
                        pltpu.VMEM((2 * tq, tk), F32),
                        pltpu.VMEM((1, 1), F32)],
        compiler_params=_params(("parallel", "parallel", "arbitrary")),
    )(slopes, lam_q1.reshape(1, -1), lam_k1.reshape(1, -1), lam_q2.reshape(1, -1), lam_k2.reshape(1, -1),
      proj, proj, proj, jnp.asarray(sel, BF16), gn.reshape(1, -1))


def _cumsum_blocks(tk):
    blocks, start = [], 0
    while start < tk:
        width = 256 if tk - start >= 256 else tk - start
        blocks.append((start, width))
        start += width
    return tuple(blocks)


def _sb_attn_kernel(q_ref, k_ref, v_ref, u_ref, o_ref, run_ref, acc_ref, *, tq, heads):
    qi = pl.program_id(2)
    q0 = pl.multiple_of(qi * tq, BLOCK)

    def attend(pieces):
        jobs = [(hd,) + tuple(piece) for piece in pieces for hd in range(heads)]
        n = len(jobs)
        z, softplus, log_beta, mask, later, w = ([None] * n for _ in range(6))
        for i, (hd, row_lo, n_rows, kstart, width, masked, first) in enumerate(jobs):
            cols = slice(hd * HEAD_W, (hd + 1) * HEAD_W)
            z[i] = _dot_nt(q_ref[0, row_lo:row_lo + n_rows, cols], k_ref[0, pl.ds(kstart, width), cols])
        for i, (hd, row_lo, n_rows, kstart, width, masked, first) in enumerate(jobs):
            neg_abs = lax.bitcast_convert_type(
                lax.bitcast_convert_type(z[i], jnp.uint32) | jnp.uint32(0x80000000), F32)
            sp = jnp.maximum(z[i], 0.0) + jnp.log(1.0 + jnp.exp2(neg_abs)) * INV_LN2
            log_beta[i] = z[i] - sp
            if masked:
                kpos = kstart + lax.broadcasted_iota(jnp.int32, (1, width), 1)
                if masked == "pad":
                    mask[i] = jnp.broadcast_to(kpos >= N_PAD, sp.shape)
                else:
                    qpos = q0 + row_lo + lax.broadcasted_iota(jnp.int32, (n_rows, 1), 0)
                    mask[i] = (qpos > kpos) if masked == "causal" else (qpos > kpos) & (kpos >= N_PAD)
                sp = jnp.where(mask[i], sp, 0.0)
            softplus[i] = sp
        for i, (hd, row_lo, n_rows, kstart, width, masked, first) in enumerate(jobs):
            rows = slice(row_lo, row_lo + n_rows)
            run = jnp.zeros((n_rows, 1), F32) if first else run_ref[hd, rows, :]
            sp16 = softplus[i].astype(BF16)
            blocks = _cumsum_blocks(width)
            parts = [None] * len(blocks)
            for idx in reversed(range(len(blocks))):
                b0, bw = blocks[idx]
                parts[idx] = _dot(sp16[:, b0:b0 + bw], u_ref[0:bw, 0:bw]) + run
                run = run - jnp.sum(softplus[i][:, b0:b0 + bw], axis=1, keepdims=True)
            run_ref[hd, rows, :] = run
            later[i] = parts[0] if len(parts) == 1 else jnp.concatenate(parts, axis=1)
        for i, (hd, row_lo, n_rows, kstart, width, masked, first) in enumerate(jobs):
            w[i] = jnp.exp2(log_beta[i] + later[i])
            if masked:
                w[i] = jnp.where(mask[i], w[i], 0.0)
            w[i] = w[i].astype(BF16)
        for i, (hd, row_lo, n_rows, kstart, width, masked, first) in enumerate(jobs):
            rows = slice(row_lo, row_lo + n_rows)
            cols = slice(hd * HEAD_W, (hd + 1) * HEAD_W)
            contrib = _dot(w[i], v_ref[0, pl.ds(kstart, width), cols])
            if first:
                acc_ref[rows, cols] = contrib
            else:
                acc_ref[rows, cols] += contrib

    def before_tile(row_lo, n_rows, kstart, width, masked):
        return [(row_lo + r, BLOCK, kstart, width, masked, False) for r in range(0, n_rows, BLOCK)]

    top = min(SB_TOP_ROWS, tq)
    n_blk = tq // BLOCK

    @pl.when(qi == 0)
    def _():
        attend([(r * BLOCK, BLOCK, q0, (r + 1) * BLOCK, "both", True) for r in range(n_blk)])

    def own(blocks):
        diag = [(r * BLOCK, BLOCK, pl.multiple_of(q0 + r * BLOCK, BLOCK), BLOCK, "causal", True) for r in blocks]
        rest = [(r * BLOCK, BLOCK, q0, r * BLOCK, False, False) for r in blocks if r > 0]
        return diag + rest

    @pl.when(qi > 0)
    def _():
        attend(own(range(top // BLOCK)))
        attend(own(range(top // BLOCK, n_blk))
               + before_tile(0, top, pl.multiple_of(q0 - SB_KEY_CHUNK, BLOCK), SB_KEY_CHUNK, False))

    def walk(row_lo, n_rows, chunks_done):
        def more_needed():
            return jnp.max(run_ref[:, row_lo:row_lo + n_rows, :]) > -ZERO_PROB_LOG2

        def chunk(kstart, width, masked):
            attend(before_tile(row_lo, n_rows, pl.multiple_of(kstart, BLOCK), width, masked))

        def cond(carry):
            kstart, go = carry
            return (kstart >= BLOCK) & go

        def body(carry):
            kstart, _ = carry
            chunk(kstart, SB_KEY_CHUNK, False)
            return kstart - SB_KEY_CHUNK, more_needed()

        kstart, go = lax.while_loop(cond, body, (q0 - (chunks_done + 1) * SB_KEY_CHUNK, more_needed()))

        def cond_tail(carry):
            kstart, go = carry
            return (kstart >= 0) & go

        def body_tail(carry):
            kstart, _ = carry
            chunk(kstart, BLOCK, "pad")
            return kstart - BLOCK, more_needed()

        lax.while_loop(cond_tail, body_tail, (kstart + SB_KEY_CHUNK - BLOCK, go))

    walk(0, top, 1)
    if top < tq:
        @pl.when(jnp.max(run_ref[:, top:, :]) > -ZERO_PROB_LOG2)
        def _():
            walk(top, tq - top, 0)

    o_ref[0] = acc_ref[...].astype(BF16)


def _sb_attention(qkv):
    b, p, _ = qkv.shape
    nh = N_SB_HEADS
    tq = _pick_tile(p, (640, 512, 256, 128))
    umax = SB_KEY_CHUNK
    jj = np.arange(umax)
    u = jnp.asarray(-(jj[:, None] > jj[None, :]).astype(np.float32), BF16)
    heads = SB_HEADS_PER_STEP
    groups = nh // heads
    gw = heads * HEAD_W
    kv_block = (1, p, gw)
    return pl.pallas_call(
        functools.partial(_sb_attn_kernel, tq=tq, heads=heads),
        grid=(b, groups, p // tq),
        in_specs=[pl.BlockSpec((1, tq, gw), lambda bi, g, i: (bi, i, g)),
                  pl.BlockSpec(kv_block, lambda bi, g, i: (bi, 0, groups + g)),
                  pl.BlockSpec(kv_block, lambda bi, g, i: (bi, 0, 2 * groups + g)),
                  _const_spec((umax, umax))],
        out_specs=pl.BlockSpec((1, tq, gw), lambda bi, g, i: (bi, i, g)),
        out_shape=jax.ShapeDtypeStruct((b, p, nh * HEAD_W), BF16),
        scratch_shapes=[pltpu.VMEM((heads, tq, 1), F32), pltpu.VMEM((tq, gw), F32)],
        compiler_params=_params(("parallel", "parallel", "arbitrary")),
    )(qkv, qkv, qkv, u)


def kernel(x, meta_tokens, mix_norm, ffn_norm, ffn_up, ffn_conv, ffn_conv_b, ffn_down, ab_w_in, ab_ret_norm,
           ab_diff_norm, ab_lam_q1, ab_lam_k1, ab_lam_q2, ab_lam_k2, ab_w_out, c_w_in, c_w_out, final_norm):
    b, s, d = x.shape
    depth = mix_norm.shape[0]
    p = s + BLOCK
    t = b * p
    h = None
    for i in range(depth):
        if i % 2 == 0:
            e = i // 2
            lambda_init = 0.8 - 0.6 * math.exp(-0.3 * i)
            w_in = ab_w_in[e].astype(BF16)
            scales = (1.0, 1.0, 1.0, 1.0, DIFF_QK_DIM ** -0.5 * LOG2_E, 1.0, 1.0)
            if i == 0:
                proj, h = _embed_norm_proj(x, meta_tokens, mix_norm[i], w_in, scales, 512)
            else:
                proj = _norm_proj(h.reshape(t, d), mix_norm[i], w_in, scales, 512).reshape(b, p, -1)
            ret = _retention(proj, ab_ret_norm[e])
            dif = _diff_attention(proj, ab_lam_q1[e], ab_lam_k1[e], ab_lam_q2[e], ab_lam_k2[e],
                                  ab_diff_norm[e], lambda_init, 4 * N_RET_HEADS)
            acts, w_out = [ret, dif], ab_w_out[e].astype(BF16)
        else:
            o = i // 2
            w_in = c_w_in[o].astype(BF16)
            n_chunks = w_in.shape[1] // 512
            q_chunks = N_SB_HEADS * HEAD_W // 512
            scales = tuple(HEAD_W ** -0.5 * LOG2_E if c < q_chunks else 1.0 for c in range(n_chunks))
            qkv = _norm_proj(h.reshape(t, d), mix_norm[i], w_in, scales, 512).reshape(b, p, -1)
            acts, w_out = [_sb_attention(qkv)], c_w_out[o].astype(BF16)
        h = _mixer_out_ffn(h, acts, w_out, ffn_norm[i], ffn_up[i].astype(BF16), ffn_conv[i], ffn_conv_b[i],
                           ffn_down[i].astype(BF16), final_norm if i == depth - 1 else None)
    return h
```

```python
import functools
import math

import numpy as np
import jax
import jax.numpy as jnp
from jax import lax
from jax.experimental import pallas as pl
from jax.experimental.pallas import tpu as pltpu

F32 = jnp.float32
BF16 = jnp.bfloat16

N_META = 16
BLOCK = 128
N_PAD = BLOCK - N_META
EPS = 1e-6
MASK_VALUE = -1e30
HEAD_W = 128
N_RET_HEADS = 4
N_DIFF_HEADS = 4
DIFF_QK_DIM = 64
N_SB_HEADS = 8
CONV_WIDTH = 3
HALO = 16
FFN_CHUNK = 768
LOG2_E = math.log2(math.e)
INV_LN2 = 1.0 / math.log(2.0)
ZERO_PROB_LOG2 = 160.0
SB_KEY_CHUNK = 256
SB_TOP_ROWS = 256
SB_HEADS_PER_STEP = 4

VMEM_LIMIT_BYTES = 56 * 1024 * 1024


def _pick_tile(n, candidates):
    for c in candidates:
        if n % c == 0:
            return c
    raise ValueError(f"no tile for {n} in {candidates}")


def _params(sem):
    return pltpu.CompilerParams(dimension_semantics=sem, vmem_limit_bytes=VMEM_LIMIT_BYTES)


def _const_spec(shape):
    return pl.BlockSpec(shape, lambda *_: (0,) * len(shape), pipeline_mode=pl.Buffered(1))


def _rms(x, g):
    return x * lax.rsqrt(jnp.mean(x * x, axis=-1, keepdims=True) + EPS) * g


def _dot(a, b):
    return jnp.dot(a, b, preferred_element_type=F32)


def _dot_nt(a, b):
    return lax.dot_general(a, b, (((1,), (1,)), ((), ())), preferred_element_type=F32)


def _dot_tn(a, b):
    return lax.dot_general(a, b, (((0,), (0,)), ((), ())), preferred_element_type=F32)


def _norm_proj_kernel(x_ref, g_ref, w_ref, o_ref, xn_ref, *, chunk, scales):
    xn_ref[...] = _rms(x_ref[...], g_ref[...]).astype(BF16)
    for c, scale in enumerate(scales):
        cols = slice(c * chunk, (c + 1) * chunk)
        r = _dot(xn_ref[...], w_ref[:, cols])
        if scale != 1.0:
            r = r * scale
        o_ref[:, cols] = r.astype(BF16)


def _norm_proj(x2, g, w, scales, chunk):
    t, d = x2.shape
    n = w.shape[1]
    tm = _pick_tile(t, (1024, 640, 512, 256, 128))
    return pl.pallas_call(
        functools.partial(_norm_proj_kernel, chunk=chunk, scales=scales),
        grid=(t // tm,),
        in_specs=[pl.BlockSpec((tm, d), lambda i: (i, 0)),
                  _const_spec((1, d)),
                  _const_spec((d, n))],
        out_specs=pl.BlockSpec((tm, n), lambda i: (i, 0)),
        out_shape=jax.ShapeDtypeStruct((t, n), BF16),
        scratch_shapes=[pltpu.VMEM((tm, d), BF16)],
        compiler_params=_params(("parallel",)),
    )(x2, g.reshape(1, d), w)


def _embed_norm_proj_kernel(x_ref, lead_ref, g_ref, w_ref, o_ref, h_ref, xn_ref, *, chunk, scales):
    tm = h_ref.shape[1]

    @pl.when(pl.program_id(1) == 0)
    def _():
        h_ref[0, 0:BLOCK, :] = lead_ref[...]
        h_ref[0, BLOCK:, :] = x_ref[0, 0:tm - BLOCK, :]

    @pl.when(pl.program_id(1) > 0)
    def _():
        h_ref[0] = x_ref[0]

    xn_ref[...] = _rms(h_ref[0], g_ref[...]).astype(BF16)
    for c, scale in enumerate(scales):
        cols = slice(c * chunk, (c + 1) * chunk)
        r = _dot(xn_ref[...], w_ref[:, cols])
        if scale != 1.0:
            r = r * scale
        o_ref[0, :, cols] = r.astype(BF16)


def _embed_norm_proj(x, meta_tokens, g, w, scales, chunk):
    b, s, d = x.shape
    p = s + BLOCK
    n = w.shape[1]
    tm = _pick_tile(p, (640, 512, 256, 128))
    lead = jnp.concatenate([jnp.zeros((N_PAD, d), x.dtype), meta_tokens.astype(x.dtype)], axis=0)
    x_spec = pl.BlockSpec((pl.Element(1), pl.Element(tm), pl.Element(d)),
                          lambda bi, i: (bi, pl.multiple_of(jnp.maximum(i * tm - BLOCK, 0), BLOCK), 0))
    return pl.pallas_call(
        functools.partial(_embed_norm_proj_kernel, chunk=chunk, scales=scales),
        grid=(b, p // tm),
        in_specs=[x_spec, _const_spec((BLOCK, d)), _const_spec((1, d)), _const_spec((d, n))],
        out_specs=[pl.BlockSpec((1, tm, n), lambda bi, i: (bi, i, 0)),
                   pl.BlockSpec((1, tm, d), lambda bi, i: (bi, i, 0))],
        out_shape=[jax.ShapeDtypeStruct((b, p, n), BF16), jax.ShapeDtypeStruct((b, p, d), F32)],
        scratch_shapes=[pltpu.VMEM((tm, d), BF16)],
        compiler_params=_params(("parallel", "parallel")),
    )(x, lead, g.reshape(1, d), w)


def _ffn_kernel(*refs, tm, f, bounds, final, p_start, act_widths):
    n_act = len(act_widths)
    x_ref, xh_ref = refs[0:2]
    act_refs = refs[2:2 + 2 * n_act]
    pos = 2 + 2 * n_act
    if n_act:
        wo_ref = refs[pos]
        pos += 1
    g_ref, wup_ref, wc_ref, bc_ref, wd_ref = refs[pos:pos + 5]
    pos += 5
    if final:
        fg_ref = refs[pos]
        pos += 1
    o_ref, xn_ref, acc_ref = refs[pos:pos + 3]
    i = pl.program_id(1)
    g = g_ref[...]
    if n_act:
        a_ref = refs[pos + 3]
        col = 0
        for k, width in enumerate(act_widths):
            a_ref[0:HALO, col:col + width] = act_refs[2 * k + 1][0]
            a_ref[HALO:, col:col + width] = act_refs[2 * k][0]
            col += width
        mix = _dot(a_ref[...], wo_ref[...])
        x_halo = xh_ref[0] + mix[0:HALO, :]
        o_ref[0] = x_ref[0] + mix[HALO:, :]
    else:
        x_halo = xh_ref[0]
        o_ref[0] = x_ref[0]
    xn_ref[0:HALO, :] = _rms(x_halo, g).astype(BF16)
    xn_ref[HALO:, :] = _rms(o_ref[0], g).astype(BF16)
    if p_start < N_PAD + HALO:
        masked_rows = HALO + N_PAD - p_start

        @pl.when(i == 0)
        def _():
            xn_ref[0:masked_rows, :] = jnp.zeros((masked_rows, xn_ref.shape[1]), BF16)

    def up_proj(c):
        lo, hi = bounds[c]
        gate = _dot(xn_ref[...], wup_ref[:, lo:hi])
        val = _dot(xn_ref[HALO:, :], wup_ref[:, f + lo:f + hi])
        return gate, val

    n_chunks = len(bounds)
    nxt = up_proj(0)
    for c in range(n_chunks):
        cols = slice(*bounds[c])
        gate, val = nxt
        if c + 1 < n_chunks:
            nxt = up_proj(c + 1)
        conv = bc_ref[:, cols]
        for tap in range(CONV_WIDTH):
            lo = HALO - (CONV_WIDTH - 1) + tap
            conv = conv + gate[lo:lo + tm, :] * wc_ref[tap:tap + 1, cols]
        half = 0.5 * conv
        mid = ((half + half * jnp.tanh(half)) * val).astype(BF16)
        contrib = _dot(mid, wd_ref[cols, :])
        if c == 0:
            acc_ref[...] = contrib
        else:
            acc_ref[...] += contrib
    y = o_ref[0] + acc_ref[...]
    if final:
        y = _rms(y, fg_ref[...])
    o_ref[0] = y


def _mixer_out_ffn(h, acts, w_out, g, w_up, w_conv, b_conv, w_down, final_g=None):
    b, p, d = h.shape
    f = w_down.shape[0]
    final = final_g is not None
    p_start = BLOCK if final else 0
    tm = _pick_tile(p - p_start, (1024, 832, 640, 512, 256, 128) if final else (832, 640, 512, 256, 128))
    bounds = tuple((lo, min(lo + FFN_CHUNK, f)) for lo in range(0, f, FFN_CHUNK))
    act_widths = tuple(a.shape[-1] for a in acts)

    def rows_spec(n_rows, back, width):
        return pl.BlockSpec(
            (pl.Element(1), pl.Element(n_rows), pl.Element(width)),
            lambda bi, i: (bi, pl.multiple_of(jnp.maximum(p_start + i * tm - back, 0), HALO), 0))

    in_specs = [rows_spec(tm, 0, d), rows_spec(HALO, HALO, d)]
    args = [h, h]
    for a, width in zip(acts, act_widths):
        in_specs += [rows_spec(tm, 0, width), rows_spec(HALO, HALO, width)]
        args += [a, a]
    in_specs.append(_const_spec(w_out.shape))
    args.append(w_out)
    in_specs += [_const_spec((1, d)),
                 _const_spec((d, 2 * f)),
                 _const_spec((CONV_WIDTH, f)),
                 _const_spec((1, f)),
                 _const_spec((f, d))]
    args += [g.reshape(1, d), w_up, w_conv, b_conv.reshape(1, f), w_down]
    if final:
        in_specs.append(_const_spec((1, d)))
        args.append(final_g.reshape(1, d))
    return pl.pallas_call(
        functools.partial(_ffn_kernel, tm=tm, f=f, bounds=bounds, final=final, p_start=p_start,
                          act_widths=act_widths),
        grid=(b, (p - p_start) // tm),
        in_specs=in_specs,
        out_specs=pl.BlockSpec((1, tm, d), lambda bi, i: (bi, i, 0)),
        out_shape=jax.ShapeDtypeStruct((b, p - p_start, d), F32),
        scratch_shapes=[pltpu.VMEM((tm + HALO, d), BF16), pltpu.VMEM((tm, d), F32),
                        pltpu.VMEM((tm + HALO, sum(act_widths)), BF16)],
        compiler_params=_params(("parallel", "parallel")),
    )(*args)


def _retention_constants():
    c = BLOCK
    hh = np.arange(N_RET_HEADS, dtype=np.float64)
    log_g = np.log1p(-(2.0 ** (-5.0 - hh)))
    j = np.arange(c, dtype=np.float64)
    rel = j[:, None] - j[None, :]
    scale = HEAD_W ** -0.5
    decay = np.where(rel >= 0, np.exp(log_g[:, None, None] * np.maximum(rel, 0.0)), 0.0) * scale
    q_decay = np.broadcast_to(np.exp(log_g[:, None] * (j + 1.0))[:, :, None], (N_RET_HEADS, c, HEAD_W))
    k_decay = np.broadcast_to((np.exp(log_g[:, None] * (c - 1.0 - j)) * scale)[:, :, None],
                              (N_RET_HEADS, c, HEAD_W))
    chunk_decay = tuple(float(v) for v in np.exp(log_g * c))
    return (jnp.asarray(decay, F32), jnp.asarray(q_decay, F32), jnp.asarray(k_decay, F32), chunk_decay)


def _retention_kernel(q_ref, k_ref, v_ref, gate_ref, dmat_ref, qd_ref, kd_ref, gn_ref, o_ref, state_ref,
                      *, tr, chunk_decay):
    i = pl.program_id(1)

    @pl.when(i == 0)
    def _():
        state_ref[...] = jnp.zeros_like(state_ref)

    n_chunks = tr // BLOCK
    units = [(c, h) for c in range(n_chunks) for h in range(N_RET_HEADS)]
    pos = i * tr + lax.broadcasted_iota(jnp.int32, (BLOCK, HEAD_W), 0)
    valid = pos >= N_PAD

    def block(c, h):
        return slice(c * BLOCK, (c + 1) * BLOCK), slice(h * HEAD_W, (h + 1) * HEAD_W)

    q, k, v, scores, kv = {}, {}, {}, {}, {}
    for c, h in units:
        rows, cols = block(c, h)
        q[c, h], k[c, h], v[c, h] = q_ref[0, rows, cols], k_ref[0, rows, cols], v_ref[0, rows, cols]
        if c == 0:
            k[c, h] = jnp.where(valid, k[c, h], jnp.zeros_like(k[c, h]))
            v[c, h] = jnp.where(valid, v[c, h], jnp.zeros_like(v[c, h]))
    for c, h in units:
        scores[c, h] = (_dot_nt(q[c, h], k[c, h]) * dmat_ref[h]).astype(BF16)
        k_scaled = (k[c, h].astype(F32) * kd_ref[h]).astype(BF16)
        kv[c, h] = _dot_tn(k_scaled, v[c, h])
    state = {}
    for h in range(N_RET_HEADS):
        state[0, h] = state_ref[h]
        for c in range(n_chunks):
            state[c + 1, h] = chunk_decay[h] * state[c, h] + kv[c, h]
        state_ref[h] = state[n_chunks, h]
    out = {}
    for c, h in units:
        out[c, h] = _dot(scores[c, h], v[c, h]) + _dot(q[c, h], state[c, h].astype(BF16)) * qd_ref[h]
    for c, h in units:
        rows, cols = block(c, h)
        o = out[c, h]
        oc = o - jnp.mean(o, axis=-1, keepdims=True)
        y = oc * lax.rsqrt(jnp.mean(oc * oc, axis=-1, keepdims=True) + EPS) * gn_ref[:, cols]
        gate = gate_ref[0, rows, cols].astype(F32)
        o_ref[0, rows, cols] = (y * (gate / (1.0 + jnp.exp(-gate)))).astype(BF16)


def _retention(proj, gn):
    b, p, _ = proj.shape
    w = N_RET_HEADS * HEAD_W
    tr = _pick_tile(p, (640, 512, 256, 128))
    dmat, qd, kd, chunk_decay = _retention_constants()

    def col_spec(cb):
        return pl.BlockSpec((1, tr, w), lambda bi, i: (bi, i, cb))

    cshape = (N_RET_HEADS, BLOCK, HEAD_W)
    return pl.pallas_call(
        functools.partial(_retention_kernel, tr=tr, chunk_decay=chunk_decay),
        grid=(b, p // tr),
        in_specs=[col_spec(0), col_spec(1), col_spec(2), col_spec(3),
                  _const_spec(cshape), _const_spec(cshape), _const_spec(cshape), _const_spec((1, w))],
        out_specs=pl.BlockSpec((1, tr, w), lambda bi, i: (bi, i, 0)),
        out_shape=jax.ShapeDtypeStruct((b, p, w), BF16),
        scratch_shapes=[pltpu.VMEM(cshape, F32)],
        compiler_params=_params(("parallel", "arbitrary")),
    )(proj, proj, proj, proj, dmat, qd, kd, gn.reshape(1, w))


def _diff_attn_kernel(slope_ref, lq1_ref, lk1_ref, lq2_ref, lk2_ref, q_ref, k_ref, v_ref, sel_ref, gn_ref, o_ref,
                      qs_ref, m_ref, l_ref, acc_ref, kmax_ref, s0_ref, s1_ref, slack_ref, *, tq, tk, rsub, lambda_init):
    h = pl.program_id(1)
    qi = pl.program_id(2)
    slope = slope_ref[h] * LOG2_E
    q0 = qi * tq
    n_sub = 2 * tq // rsub

    @pl.when(qi == 0)
    def _():
        kf = k_ref[0].astype(F32)
        norms = _dot((kf * kf).astype(BF16), sel_ref[...])
        kmax_ref[...] = jnp.sqrt(jnp.max(norms, axis=0, keepdims=True)) * 1.02

    q = q_ref[0]
    lane = lax.broadcasted_iota(jnp.int32, (tq, HEAD_W), 1)
    qs_ref[0:tq, :] = jnp.where(lane < DIFF_QK_DIM, q, jnp.zeros_like(q))
    qs_ref[tq:, :] = jnp.where(lane >= DIFF_QK_DIM, q, jnp.zeros_like(q))
    m_ref[...] = jnp.full_like(m_ref, MASK_VALUE)
    l_ref[...] = jnp.zeros_like(l_ref)
    acc_ref[...] = jnp.zeros_like(acc_ref)

    def tile_start(j):
        return j * tk if isinstance(j, int) else pl.multiple_of(j * tk, tk)

    s_refs = (s0_ref, s1_ref)

    def issue_scores(j, slot, per_block=False):
        k = k_ref[0, pl.ds(tile_start(j), tk), :]
        if per_block:
            for r in range(n_sub):
                rows = slice(r * rsub, (r + 1) * rsub)
                s_refs[slot][rows, :] = _dot_nt(qs_ref[rows, :], k)
        else:
            s_refs[slot][...] = _dot_nt(qs_ref[...], k)

    def consume(j, slot, mode, prefetch):
        s_ref = s_refs[slot]
        if mode == "diag":
            issue_scores(j, slot, per_block=True)
        if prefetch is not None:
            issue_scores(prefetch, 1 - slot)
        start = tile_start(j)
        kpos = j * tk + lax.broadcasted_iota(jnp.int32, (1, tk), 1)
        bias = slope * (kpos - q0).astype(F32)
        v = v_ref[0, pl.ds(start, tk), :]

        def stage_b(r):
            rows = slice(r * rsub, (r + 1) * rsub)
            s = s_ref[rows, :] + bias
            if mode == "first":
                s = jnp.where(kpos >= N_PAD, s, MASK_VALUE)
            elif mode == "diag":
                qpos = q0 + (r * rsub) % tq + lax.broadcasted_iota(jnp.int32, (rsub, 1), 0)
                s = jnp.where((qpos >= kpos) & (kpos >= N_PAD), s, MASK_VALUE)
            m_prev = m_ref[rows, :]
            m_new = jnp.maximum(m_prev, jnp.max(s, axis=1, keepdims=True))
            p = jnp.exp2(s - m_new)
            alpha = jnp.exp2(m_prev - m_new)
            l_ref[rows, :] = alpha * l_ref[rows, :] + jnp.sum(p, axis=1, keepdims=True)
            m_ref[rows, :] = m_new
            return p.astype(BF16), alpha

        def stage_c(r, p, alpha):
            rows = slice(r * rsub, (r + 1) * rsub)
            acc_ref[rows, :] = alpha * acc_ref[rows, :] + _dot(p, v)

        b_out = {}
        for t in range(n_sub + 1):
            if t < n_sub:
                b_out[t] = stage_b(t)
            if t >= 1:
                stage_c(t - 1, *b_out.pop(t - 1))

    @pl.when(qi >= 0)
    def _():
        qf = qs_ref[...].astype(F32)
        qnorm = jnp.sqrt(jnp.sum(qf * qf, axis=1, keepdims=True))
        row = lax.broadcasted_iota(jnp.int32, (2 * tq, 1), 0)
        bound = qnorm * jnp.where(row < tq, kmax_ref[:, 0:1], kmax_ref[:, 1:2])
        consume(qi, 0, "diag", jnp.maximum(qi - 1, 0))
        slack_ref[...] = jnp.max(bound - m_ref[...], axis=0, keepdims=True)

    slack = jnp.max(slack_ref[...])

    def needed(rel):
        return slack + slope * rel >= -ZERO_PROB_LOG2

    def cond(carry):
        j, rel = carry
        return (j >= 2) & needed(rel - tk)

    def body(carry):
        j, rel = carry
        consume(j, 1, "plain", j - 1)
        consume(j - 1, 0, "plain", j - 2)
        return j - 2, rel - 2 * tk

    j_mid, rel_mid = lax.while_loop(cond, body, (qi - 1, jnp.float32(-1.0)))
    single = (j_mid >= 1) & needed(rel_mid)

    @pl.when(single)
    def _():
        consume(j_mid, 1, "plain", j_mid - 1)

    j_end = jnp.where(single, j_mid - 1, j_mid)
    rel_end = jnp.where(single, rel_mid - tk, rel_mid)
    for static_slot, in_this_slot in ((0, single), (1, jnp.logical_not(single))):
        @pl.when((j_end == 0) & needed(rel_end) & in_this_slot)
        def _():
            consume(0, static_slot, "first", None)

    o = acc_ref[...] / l_ref[...]
    lam = (jnp.exp(jnp.sum(lq1_ref[...] * lk1_ref[...], axis=1, keepdims=True))
           - jnp.exp(jnp.sum(lq2_ref[...] * lk2_ref[...], axis=1, keepdims=True)) + lambda_init)
    o = o[0:tq, :] - lam * o[tq:, :]
    o_ref[0] = (_rms(o, gn_ref[...]) * (1.0 - lambda_init)).astype(BF16)


def _diff_attention(proj, lam_q1, lam_k1, lam_q2, lam_k2, gn, lambda_init, col0):
    b, p, _ = proj.shape
    nh = N_DIFF_HEADS
    tq = tk = _pick_tile(p, (640, 512, 256, 128))
    rsub = tq
    slopes = jnp.asarray(2.0 ** (-8.0 * (np.arange(nh) + 1.0) / nh), F32)
    lam_spec = _const_spec((1, DIFF_QK_DIM))
    kv_block = (1, p, HEAD_W)
    sel = np.zeros((HEAD_W, HEAD_W), np.float32)
    sel[:DIFF_QK_DIM, 0] = 1.0
    sel[DIFF_QK_DIM:, 1] = 1.0
    return pl.pallas_call(
        functools.partial(_diff_attn_kernel, tq=tq, tk=tk, rsub=rsub, lambda_init=lambda_init),
        grid=(b, nh, p // tq),
        in_specs=[pl.BlockSpec(memory_space=pltpu.SMEM),
                  lam_spec, lam_spec, lam_spec, lam_spec,
                  pl.BlockSpec((1, tq, HEAD_W), lambda bi, h, i: (bi, i, col0 + h)),
                  pl.BlockSpec(kv_block, lambda bi, h, i: (bi, 0, col0 + nh + h)),
                  pl.BlockSpec(kv_block, lambda bi, h, i: (bi, 0, col0 + 2 * nh + h)),
                  _const_spec((HEAD_W, HEAD_W)),
                  pl.BlockSpec((1, HEAD_W), lambda bi, h, i: (0, h))],
        out_specs=pl.BlockSpec((1, tq, HEAD_W), lambda bi, h, i: (bi, i, h)),
        out_shape=jax.ShapeDtypeStruct((b, p, nh * HEAD_W), BF16),
        scratch_shapes=[pltpu.VMEM((2 * tq, HEAD_W), BF16),
                        pltpu.VMEM((2 * tq, 1), F32),
                        pltpu.VMEM((2 * tq, 1), F32),
                        pltpu.VMEM((2 * tq, HEAD_W), F32),
                        pltpu.VMEM((1, HEAD_W), F32),
                        pltpu.VMEM((2 * tq, tk), F32),
                        pltpu.VMEM((2 * tq, tk), F32),
                        pltpu.VMEM((1, 1), F32)],
        compiler_params=_params(("parallel", "parallel", "arbitrary")),
    )(slopes, lam_q1.reshape(1, -1), lam_k1.reshape(1, -1), lam_q2.reshape(1, -1), lam_k2.reshape(1, -1),
      proj, proj, proj, jnp.asarray(sel, BF16), gn.reshape(1, -1))


def _cumsum_blocks(tk):
    blocks, start = [], 0
    while start < tk:
        width = 256 if tk - start >= 256 else tk - start
        blocks.append((start, width))
        start += width
    return tuple(blocks)


def _sb_attn_kernel(q_ref, k_ref, v_ref, u_ref, o_ref, run_ref, acc_ref, *, tq, heads):
    qi = pl.program_id(2)
    q0 = pl.multiple_of(qi * tq, BLOCK)

    def attend(pieces):
        jobs = [(hd,) + tuple(piece) for piece in pieces for hd in range(heads)]
        n = len(jobs)
        z, softplus, log_beta, mask, later, w = ([None] * n for _ in range(6))
        for i, (hd, row_lo, n_rows, kstart, width, masked, first) in enumerate(jobs):
            cols = slice(hd * HEAD_W, (hd + 1) * HEAD_W)
            z[i] = _dot_nt(q_ref[0, row_lo:row_lo + n_rows, cols], k_ref[0, pl.ds(kstart, width), cols])
        for i, (hd, row_lo, n_rows, kstart, width, masked, first) in enumerate(jobs):
            neg_abs = lax.bitcast_convert_type(
                lax.bitcast_convert_type(z[i], jnp.uint32) | jnp.uint32(0x80000000), F32)
            sp = jnp.maximum(z[i], 0.0) + jnp.log(1.0 + jnp.exp2(neg_abs)) * INV_LN2
            log_beta[i] = z[i] - sp
            if masked:
                kpos = kstart + lax.broadcasted_iota(jnp.int32, (1, width), 1)
                if masked == "pad":
                    mask[i] = jnp.broadcast_to(kpos >= N_PAD, sp.shape)
                else:
                    qpos = q0 + row_lo + lax.broadcasted_iota(jnp.int32, (n_rows, 1), 0)
                    mask[i] = (qpos > kpos) if masked == "causal" else (qpos > kpos) & (kpos >= N_PAD)
                sp = jnp.where(mask[i], sp, 0.0)
            softplus[i] = sp
        for i, (hd, row_lo, n_rows, kstart, width, masked, first) in enumerate(jobs):
            rows = slice(row_lo, row_lo + n_rows)
            run = jnp.zeros((n_rows, 1), F32) if first else run_ref[hd, rows, :]
            sp16 = softplus[i].astype(BF16)
            blocks = _cumsum_blocks(width)
            parts = [None] * len(blocks)
            for idx in reversed(range(len(blocks))):
                b0, bw = blocks[idx]
                parts[idx] = _dot(sp16[:, b0:b0 + bw], u_ref[0:bw, 0:bw]) + run
                run = run - jnp.sum(softplus[i][:, b0:b0 + bw], axis=1, keepdims=True)
            run_ref[hd, rows, :] = run
            later[i] = parts[0] if len(parts) == 1 else jnp.concatenate(parts, axis=1)
        for i, (hd, row_lo, n_rows, kstart, width, masked, first) in enumerate(jobs):
            w[i] = jnp.exp2(log_beta[i] + later[i])
            if masked:
                w[i] = jnp.where(mask[i], w[i], 0.0)
            w[i] = w[i].astype(BF16)
        for i, (hd, row_lo, n_rows, kstart, width, masked, first) in enumerate(jobs):
            rows = slice(row_lo, row_lo + n_rows)
            cols = slice(hd * HEAD_W, (hd + 1) * HEAD_W)
            contrib = _dot(w[i], v_ref[0, pl.ds(kstart, width), cols])
            if first:
                acc_ref[rows, cols] = contrib
            else:
                acc_ref[rows, cols] += contrib

    def before_tile(row_lo, n_rows, kstart, width, masked):
        return [(row_lo + r, BLOCK, kstart, width, masked, False) for r in range(0, n_rows, BLOCK)]

    top = min(SB_TOP_ROWS, tq)
    n_blk = tq // BLOCK

    @pl.when(qi == 0)
    def _():
        attend([(r * BLOCK, BLOCK, q0, (r + 1) * BLOCK, "both", True) for r in range(n_blk)])

    def own(blocks):
        diag = [(r * BLOCK, BLOCK, pl.multiple_of(q0 + r * BLOCK, BLOCK), BLOCK, "causal", True) for r in blocks]
        rest = [(r * BLOCK, BLOCK, q0, r * BLOCK, False, False) for r in blocks if r > 0]
        return diag + rest

    @pl.when(qi > 0)
    def _():
        attend(own(range(top // BLOCK)))
        attend(own(range(top // BLOCK, n_blk))
               + before_tile(0, top, pl.multiple_of(q0 - SB_KEY_CHUNK, BLOCK), SB_KEY_CHUNK, False))

    def walk(row_lo, n_rows, chunks_done):
        def more_needed():
            return jnp.max(run_ref[:, row_lo:row_lo + n_rows, :]) > -ZERO_PROB_LOG2

        def chunk(kstart, width, masked):
            attend(before_tile(row_lo, n_rows, pl.multiple_of(kstart, BLOCK), width, masked))

        def cond(carry):
            kstart, go = carry
            return (kstart >= BLOCK) & go

        def body(carry):
            kstart, _ = carry
            chunk(kstart, SB_KEY_CHUNK, False)
            return kstart - SB_KEY_CHUNK, more_needed()

        kstart, go = lax.while_loop(cond, body, (q0 - (chunks_done + 1) * SB_KEY_CHUNK, more_needed()))

        def cond_tail(carry):
            kstart, go = carry
            return (kstart >= 0) & go

        def body_tail(carry):
            kstart, _ = carry
            chunk(kstart, BLOCK, "pad")
            return kstart - BLOCK, more_needed()

        lax.while_loop(cond_tail, body_tail, (kstart + SB_KEY_CHUNK - BLOCK, go))

    walk(0, top, 1)
    if top < tq:
        @pl.when(jnp.max(run_ref[:, top:, :]) > -ZERO_PROB_LOG2)
        def _():
            walk(top, tq - top, 0)

    o_ref[0] = acc_ref[...].astype(BF16)


def _sb_attention(qkv):
    b, p, _ = qkv.shape
    nh = N_SB_HEADS
    tq = _pick_tile(p, (640, 512, 256, 128))
    umax = SB_KEY_CHUNK
    jj = np.arange(umax)
    u = jnp.asarray(-(jj[:, None] > jj[None, :]).astype(np.float32), BF16)
    heads = SB_HEADS_PER_STEP
    groups = nh // heads
    gw = heads * HEAD_W
    kv_block = (1, p, gw)
    return pl.pallas_call(
        functools.partial(_sb_attn_kernel, tq=tq, heads=heads),
        grid=(b, groups, p // tq),
        in_specs=[pl.BlockSpec((1, tq, gw), lambda bi, g, i: (bi, i, g)),
                  pl.BlockSpec(kv_block, lambda bi, g, i: (bi, 0, groups + g)),
                  pl.BlockSpec(kv_block, lambda bi, g, i: (bi, 0, 2 * groups + g)),
                  _const_spec((umax, umax))],
        out_specs=pl.BlockSpec((1, tq, gw), lambda bi, g, i: (bi, i, g)),
        out_shape=jax.ShapeDtypeStruct((b, p, nh * HEAD_W), BF16),
        scratch_shapes=[pltpu.VMEM((heads, tq, 1), F32), pltpu.VMEM((tq, gw), F32)],
        compiler_params=_params(("parallel", "parallel", "arbitrary")),
    )(qkv, qkv, qkv, u)


def kernel(x, meta_tokens, mix_norm, ffn_norm, ffn_up, ffn_conv, ffn_conv_b, ffn_down, ab_w_in, ab_ret_norm,
           ab_diff_norm, ab_lam_q1, ab_lam_k1, ab_lam_q2, ab_lam_k2, ab_w_out, c_w_in, c_w_out, final_norm):
    b, s, d = x.shape
    depth = mix_norm.shape[0]
    p = s + BLOCK
    t = b * p
    h = None
    for i in range(depth):
        if i % 2 == 0:
            e = i // 2
            lambda_init = 0.8 - 0.6 * math.exp(-0.3 * i)
            w_in = ab_w_in[e].astype(BF16)
            scales = (1.0, 1.0, 1.0, 1.0, DIFF_QK_DIM ** -0.5 * LOG2_E, 1.0, 1.0)
            if i == 0:
                proj, h = _embed_norm_proj(x, meta_tokens, mix_norm[i], w_in, scales, 512)
            else:
                proj = _norm_proj(h.reshape(t, d), mix_norm[i], w_in, scales, 512).reshape(b, p, -1)
            ret = _retention(proj, ab_ret_norm[e])
            dif = _diff_attention(proj, ab_lam_q1[e], ab_lam_k1[e], ab_lam_q2[e], ab_lam_k2[e],
                                  ab_diff_norm[e], lambda_init, 4 * N_RET_HEADS)
            acts, w_out = [ret, dif], ab_w_out[e].astype(BF16)
        else:
            o = i // 2
            w_in = c_w_in[o].astype(BF16)
            n_chunks = w_in.shape[1] // 512
            q_chunks = N_SB_HEADS * HEAD_W // 512
            scales = tuple(HEAD_W ** -0.5 * LOG2_E if c < q_chunks else 1.0 for c in range(n_chunks))
            qkv = _norm_proj(h.reshape(t, d), mix_norm[i], w_in, scales, 512).reshape(b, p, -1)
            acts, w_out = [_sb_attention(qkv)], c_w_out[o].astype(BF16)
        h = _mixer_out_ffn(h, acts, w_out, ffn_norm[i], ffn_up[i].astype(BF16), ffn_conv[i], ffn_conv_b[i],
                           ffn_down[i].astype(BF16), final_norm if i == depth - 1 else None)
    return h
```

```python
import functools
import math

import numpy as np
import jax
import jax.numpy as jnp
from jax import lax
from jax.experimental import pallas as pl
from jax.experimental.pallas import tpu as pltpu

F32 = jnp.float32
BF16 = jnp.bfloat16

N_META = 16
BLOCK = 128
N_PAD = BLOCK - N_META
EPS = 1e-6
MASK_VALUE = -1e30
HEAD_W = 128
N_RET_HEADS = 4
N_DIFF_HEADS = 4
DIFF_QK_DIM = 64
N_SB_HEADS = 8
CONV_WIDTH = 3
HALO = 16
FFN_CHUNK = 1024
LOG2_E = math.log2(math.e)
INV_LN2 = 1.0 / math.log(2.0)
ZERO_PROB_LOG2 = 160.0
SB_KEY_CHUNK = 256
SB_TOP_ROWS = 256
SB_HEADS_PER_STEP = 4

VMEM_LIMIT_BYTES = 56 * 1024 * 1024


def _pick_tile(n, candidates):
    for c in candidates:
        if n % c == 0:
            return c
    raise ValueError(f"no tile for {n} in {candidates}")


def _params(sem):
    return pltpu.CompilerParams(dimension_semantics=sem, vmem_limit_bytes=VMEM_LIMIT_BYTES)


def _const_spec(shape):
    return pl.BlockSpec(shape, lambda *_: (0,) * len(shape), pipeline_mode=pl.Buffered(1))


def _rms(x, g):
    return x * lax.rsqrt(jnp.mean(x * x, axis=-1, keepdims=True) + EPS) * g


def _dot(a, b):
    return jnp.dot(a, b, preferred_element_type=F32)


def _dot_nt(a, b):
    return lax.dot_general(a, b, (((1,), (1,)), ((), ())), preferred_element_type=F32)


def _dot_tn(a, b):
    return lax.dot_general(a, b, (((0,), (0,)), ((), ())), preferred_element_type=F32)


def _norm_proj_kernel(x_ref, g_ref, w_ref, o_ref, xn_ref, *, chunk, scales):
    xn_ref[...] = _rms(x_ref[...], g_ref[...]).astype(BF16)
    for c, scale in enumerate(scales):
        cols = slice(c * chunk, (c + 1) * chunk)
        r = _dot(xn_ref[...], w_ref[:, cols])
        if scale != 1.0:
            r = r * scale
        o_ref[:, cols] = r.astype(BF16)


def _norm_proj(x2, g, w, scales, chunk):
    t, d = x2.shape
    n = w.shape[1]
    tm = _pick_tile(t, (1024, 640, 512, 256, 128))
    return pl.pallas_call(
        functools.partial(_norm_proj_kernel, chunk=chunk, scales=scales),
        grid=(t // tm,),
        in_specs=[pl.BlockSpec((tm, d), lambda i: (i, 0)),
                  _const_spec((1, d)),
                  _const_spec((d, n))],
        out_specs=pl.BlockSpec((tm, n), lambda i: (i, 0)),
        out_shape=jax.ShapeDtypeStruct((t, n), BF16),
        scratch_shapes=[pltpu.VMEM((tm, d), BF16)],
        compiler_params=_params(("parallel",)),
    )(x2, g.reshape(1, d), w)


def _embed_norm_proj_kernel(x_ref, lead_ref, g_ref, w_ref, o_ref, h_ref, xn_ref, *, chunk, scales):
    tm = h_ref.shape[1]

    @pl.when(pl.program_id(1) == 0)
    def _():
        h_ref[0, 0:BLOCK, :] = lead_ref[...]
        h_ref[0, BLOCK:, :] = x_ref[0, 0:tm - BLOCK, :]

    @pl.when(pl.program_id(1) > 0)
    def _():
        h_ref[0] = x_ref[0]

    xn_ref[...] = _rms(h_ref[0], g_ref[...]).astype(BF16)
    for c, scale in enumerate(scales):
        cols = slice(c * chunk, (c + 1) * chunk)
        r = _dot(xn_ref[...], w_ref[:, cols])
        if scale != 1.0:
            r = r * scale
        o_ref[0, :, cols] = r.astype(BF16)


def _embed_norm_proj(x, meta_tokens, g, w, scales, chunk):
    b, s, d = x.shape
    p = s + BLOCK
    n = w.shape[1]
    tm = _pick_tile(p, (640, 512, 256, 128))
    lead = jnp.concatenate([jnp.zeros((N_PAD, d), x.dtype), meta_tokens.astype(x.dtype)], axis=0)
    x_spec = pl.BlockSpec((pl.Element(1), pl.Element(tm), pl.Element(d)),
                          lambda bi, i: (bi, pl.multiple_of(jnp.maximum(i * tm - BLOCK, 0), BLOCK), 0))
    return pl.pallas_call(
        functools.partial(_embed_norm_proj_kernel, chunk=chunk, scales=scales),
        grid=(b, p // tm),
        in_specs=[x_spec, _const_spec((BLOCK, d)), _const_spec((1, d)), _const_spec((d, n))],
        out_specs=[pl.BlockSpec((1, tm, n), lambda bi, i: (bi, i, 0)),
                   pl.BlockSpec((1, tm, d), lambda bi, i: (bi, i, 0))],
        out_shape=[jax.ShapeDtypeStruct((b, p, n), BF16), jax.ShapeDtypeStruct((b, p, d), F32)],
        scratch_shapes=[pltpu.VMEM((tm, d), BF16)],
        compiler_params=_params(("parallel", "parallel")),
    )(x, lead, g.reshape(1, d), w)


def _ffn_kernel(*refs, tm, f, bounds, final, p_start, act_widths):
    n_act = len(act_widths)
    x_ref, xh_ref = refs[0:2]
    act_refs = refs[2:2 + 2 * n_act]
    pos = 2 + 2 * n_act
    if n_act:
        wo_ref = refs[pos]
        pos += 1
    g_ref, wup_ref, wc_ref, bc_ref, wd_ref = refs[pos:pos + 5]
    pos += 5
    if final:
        fg_ref = refs[pos]
        pos += 1
    o_ref, xn_ref, acc_ref = refs[pos:pos + 3]
    i = pl.program_id(1)
    g = g_ref[...]
    if n_act:
        a_ref = refs[pos + 3]
        col = 0
        for k, width in enumerate(act_widths):
            a_ref[0:HALO, col:col + width] = act_refs[2 * k + 1][0]
            a_ref[HALO:, col:col + width] = act_refs[2 * k][0]
            col += width
        mix = _dot(a_ref[...], wo_ref[...])
        x_halo = xh_ref[0] + mix[0:HALO, :]
        o_ref[0] = x_ref[0] + mix[HALO:, :]
    else:
        x_halo = xh_ref[0]
        o_ref[0] = x_ref[0]
    xn_ref[0:HALO, :] = _rms(x_halo, g).astype(BF16)
    xn_ref[HALO:, :] = _rms(o_ref[0], g).astype(BF16)
    if p_start < N_PAD + HALO:
        masked_rows = HALO + N_PAD - p_start

        @pl.when(i == 0)
        def _():
            xn_ref[0:masked_rows, :] = jnp.zeros((masked_rows, xn_ref.shape[1]), BF16)

    def up_proj(c):
        lo, hi = bounds[c]
        gate = _dot(xn_ref[...], wup_ref[:, lo:hi])
        val = _dot(xn_ref[HALO:, :], wup_ref[:, f + lo:f + hi])
        return gate, val

    n_chunks = len(bounds)
    nxt = up_proj(0)
    for c in range(n_chunks):
        cols = slice(*bounds[c])
        gate, val = nxt
        if c + 1 < n_chunks:
            nxt = up_proj(c + 1)
        conv = bc_ref[:, cols]
        for tap in range(CONV_WIDTH):
            lo = HALO - (CONV_WIDTH - 1) + tap
            conv = conv + gate[lo:lo + tm, :] * wc_ref[tap:tap + 1, cols]
        half = 0.5 * conv
        mid = ((half + half * jnp.tanh(half)) * val).astype(BF16)
        contrib = _dot(mid, wd_ref[cols, :])
        if c == 0:
            acc_ref[...] = contrib
        else:
            acc_ref[...] += contrib
    y = o_ref[0] + acc_ref[...]
    if final:
        y = _rms(y, fg_ref[...])
    o_ref[0] = y


def _mixer_out_ffn(h, acts, w_out, g, w_up, w_conv, b_conv, w_down, final_g=None):
    b, p, d = h.shape
    f = w_down.shape[0]
    final = final_g is not None
    p_start = BLOCK if final else 0
    tm = _pick_tile(p - p_start, (1024, 832, 640, 512, 256, 128) if final else (832, 640, 512, 256, 128))
    bounds = tuple((lo, min(lo + FFN_CHUNK, f)) for lo in range(0, f, FFN_CHUNK))
    act_widths = tuple(a.shape[-1] for a in acts)

    def rows_spec(n_rows, back, width):
        return pl.BlockSpec(
            (pl.Element(1), pl.Element(n_rows), pl.Element(width)),
            lambda bi, i: (bi, pl.multiple_of(jnp.maximum(p_start + i * tm - back, 0), HALO), 0))

    in_specs = [rows_spec(tm, 0, d), rows_spec(HALO, HALO, d)]
    args = [h, h]
    for a, width in zip(acts, act_widths):
        in_specs += [rows_spec(tm, 0, width), rows_spec(HALO, HALO, width)]
        args += [a, a]
    in_specs.append(_const_spec(w_out.shape))
    args.append(w_out)
    in_specs += [_const_spec((1, d)),
                 _const_spec((d, 2 * f)),
                 _const_spec((CONV_WIDTH, f)),
                 _const_spec((1, f)),
                 _const_spec((f, d))]
    args += [g.reshape(1, d), w_up, w_conv, b_conv.reshape(1, f), w_down]
    if final:
        in_specs.append(_const_spec((1, d)))
        args.append(final_g.reshape(1, d))
    return pl.pallas_call(
        functools.partial(_ffn_kernel, tm=tm, f=f, bounds=bounds, final=final, p_start=p_start,
                          act_widths=act_widths),
        grid=(b, (p - p_start) // tm),
        in_specs=in_specs,
        out_specs=pl.BlockSpec((1, tm, d), lambda bi, i: (bi, i, 0)),
        out_shape=jax.ShapeDtypeStruct((b, p - p_start, d), F32),
        scratch_shapes=[pltpu.VMEM((tm + HALO, d), BF16), pltpu.VMEM((tm, d), F32),
                        pltpu.VMEM((tm + HALO, sum(act_widths)), BF16)],
        compiler_params=_params(("parallel", "parallel")),
    )(*args)


def _retention_constants():
    c = BLOCK
    hh = np.arange(N_RET_HEADS, dtype=np.float64)
    log_g = np.log1p(-(2.0 ** (-5.0 - hh)))
    j = np.arange(c, dtype=np.float64)
    rel = j[:, None] - j[None, :]
    scale = HEAD_W ** -0.5
    decay = np.where(rel >= 0, np.exp(log_g[:, None, None] * np.maximum(rel, 0.0)), 0.0) * scale
    q_decay = np.broadcast_to(np.exp(log_g[:, None] * (j + 1.0))[:, :, None], (N_RET_HEADS, c, HEAD_W))
    k_decay = np.broadcast_to((np.exp(log_g[:, None] * (c - 1.0 - j)) * scale)[:, :, None],
                              (N_RET_HEADS, c, HEAD_W))
    chunk_decay = tuple(float(v) for v in np.exp(log_g * c))
    return (jnp.asarray(decay, F32), jnp.asarray(q_decay, F32), jnp.asarray(k_decay, F32), chunk_decay)


def _retention_kernel(q_ref, k_ref, v_ref, gate_ref, dmat_ref, qd_ref, kd_ref, gn_ref, o_ref, state_ref,
                      *, tr, chunk_decay):
    i = pl.program_id(1)

    @pl.when(i == 0)
    def _():
        state_ref[...] = jnp.zeros_like(state_ref)

    n_chunks = tr // BLOCK
    units = [(c, h) for c in range(n_chunks) for h in range(N_RET_HEADS)]
    pos = i * tr + lax.broadcasted_iota(jnp.int32, (BLOCK, HEAD_W), 0)
    valid = pos >= N_PAD

    def block(c, h):
        return slice(c * BLOCK, (c + 1) * BLOCK), slice(h * HEAD_W, (h + 1) * HEAD_W)

    q, k, v, scores, kv = {}, {}, {}, {}, {}
    for c, h in units:
        rows, cols = block(c, h)
        q[c, h], k[c, h], v[c, h] = q_ref[0, rows, cols], k_ref[0, rows, cols], v_ref[0, rows, cols]
        if c == 0:
            k[c, h] = jnp.where(valid, k[c, h], jnp.zeros_like(k[c, h]))
            v[c, h] = jnp.where(valid, v[c, h], jnp.zeros_like(v[c, h]))
    for c, h in units:
        scores[c, h] = (_dot_nt(q[c, h], k[c, h]) * dmat_ref[h]).astype(BF16)
        k_scaled = (k[c, h].astype(F32) * kd_ref[h]).astype(BF16)
        kv[c, h] = _dot_tn(k_scaled, v[c, h])
    state = {}
    for h in range(N_RET_HEADS):
        state[0, h] = state_ref[h]
        for c in range(n_chunks):
            state[c + 1, h] = chunk_decay[h] * state[c, h] + kv[c, h]
        state_ref[h] = state[n_chunks, h]
    out = {}
    for c, h in units:
        out[c, h] = _dot(scores[c, h], v[c, h]) + _dot(q[c, h], state[c, h].astype(BF16)) * qd_ref[h]
    for c, h in units:
        rows, cols = block(c, h)
        o = out[c, h]
        oc = o - jnp.mean(o, axis=-1, keepdims=True)
        y = oc * lax.rsqrt(jnp.mean(oc * oc, axis=-1, keepdims=True) + EPS) * gn_ref[:, cols]
        gate = gate_ref[0, rows, cols].astype(F32)
        o_ref[0, rows, cols] = (y * (gate / (1.0 + jnp.exp(-gate)))).astype(BF16)


def _retention(proj, gn):
    b, p, _ = proj.shape
    w = N_RET_HEADS * HEAD_W
    tr = _pick_tile(p, (640, 512, 256, 128))
    dmat, qd, kd, chunk_decay = _retention_constants()

    def col_spec(cb):
        return pl.BlockSpec((1, tr, w), lambda bi, i: (bi, i, cb))

    cshape = (N_RET_HEADS, BLOCK, HEAD_W)
    return pl.pallas_call(
        functools.partial(_retention_kernel, tr=tr, chunk_decay=chunk_decay),
        grid=(b, p // tr),
        in_specs=[col_spec(0), col_spec(1), col_spec(2), col_spec(3),
                  _const_spec(cshape), _const_spec(cshape), _const_spec(cshape), _const_spec((1, w))],
        out_specs=pl.BlockSpec((1, tr, w), lambda bi, i: (bi, i, 0)),
        out_shape=jax.ShapeDtypeStruct((b, p, w), BF16),
        scratch_shapes=[pltpu.VMEM(cshape, F32)],
        compiler_params=_params(("parallel", "arbitrary")),
    )(proj, proj, proj, proj, dmat, qd, kd, gn.reshape(1, w))


def _diff_attn_kernel(slope_ref, lq1_ref, lk1_ref, lq2_ref, lk2_ref, q_ref, k_ref, v_ref, sel_ref, gn_ref, o_ref,
                      qs_ref, m_ref, l_ref, acc_ref, kmax_ref, s0_ref, s1_ref, slack_ref, *, tq, tk, rsub, lambda_init):
    h = pl.program_id(1)
    qi = pl.program_id(2)
    slope = slope_ref[h] * LOG2_E
    q0 = qi * tq
    n_sub = 2 * tq // rsub

    @pl.when(qi == 0)
    def _():
        kf = k_ref[0].astype(F32)
        norms = _dot((kf * kf).astype(BF16), sel_ref[...])
        kmax_ref[...] = jnp.sqrt(jnp.max(norms, axis=0, keepdims=True)) * 1.02

    q = q_ref[0]
    lane = lax.broadcasted_iota(jnp.int32, (tq, HEAD_W), 1)
    qs_ref[0:tq, :] = jnp.where(lane < DIFF_QK_DIM, q, jnp.zeros_like(q))
    qs_ref[tq:, :] = jnp.where(lane >= DIFF_QK_DIM, q, jnp.zeros_like(q))
    m_ref[...] = jnp.full_like(m_ref, MASK_VALUE)
    l_ref[...] = jnp.zeros_like(l_ref)
    acc_ref[...] = jnp.zeros_like(acc_ref)

    def tile_start(j):
        return j * tk if isinstance(j, int) else pl.multiple_of(j * tk, tk)

    s_refs = (s0_ref, s1_ref)

    def issue_scores(j, slot, per_block=False):
        k = k_ref[0, pl.ds(tile_start(j), tk), :]
        if per_block:
            for r in range(n_sub):
                rows = slice(r * rsub, (r + 1) * rsub)
                s_refs[slot][rows, :] = _dot_nt(qs_ref[rows, :], k)
        else:
            s_refs[slot][...] = _dot_nt(qs_ref[...], k)

    def consume(j, slot, mode, prefetch):
        s_ref = s_refs[slot]
        if mode == "diag":
            issue_scores(j, slot, per_block=True)
        if prefetch is not None:
            issue_scores(prefetch, 1 - slot)
        start = tile_start(j)
        kpos = j * tk + lax.broadcasted_iota(jnp.int32, (1, tk), 1)
        bias = slope * (kpos - q0).astype(F32)
        v = v_ref[0, pl.ds(start, tk), :]

        def stage_b(r):
            rows = slice(r * rsub, (r + 1) * rsub)
            s = s_ref[rows, :] + bias
            if mode == "first":
                s = jnp.where(kpos >= N_PAD, s, MASK_VALUE)
            elif mode == "diag":
                qpos = q0 + (r * rsub) % tq + lax.broadcasted_iota(jnp.int32, (rsub, 1), 0)
                s = jnp.where((qpos >= kpos) & (kpos >= N_PAD), s, MASK_VALUE)
            m_prev = m_ref[rows, :]
            m_new = jnp.maximum(m_prev, jnp.max(s, axis=1, keepdims=True))
            p = jnp.exp2(s - m_new)
            alpha = jnp.exp2(m_prev - m_new)
            l_ref[rows, :] = alpha * l_ref[rows, :] + jnp.sum(p, axis=1, keepdims=True)
            m_ref[rows, :] = m_new
            return p.astype(BF16), alpha

        def stage_c(r, p, alpha):
            rows = slice(r * rsub, (r + 1) * rsub)
            acc_ref[rows, :] = alpha * acc_ref[rows, :] + _dot(p, v)

        b_out = {}
        for t in range(n_sub + 1):
            if t < n_sub:
                b_out[t] = stage_b(t)
            if t >= 1:
                stage_c(t - 1, *b_out.pop(t - 1))

    @pl.when(qi >= 0)
    def _():
        qf = qs_ref[...].astype(F32)
        qnorm = jnp.sqrt(jnp.sum(qf * qf, axis=1, keepdims=True))
        row = lax.broadcasted_iota(jnp.int32, (2 * tq, 1), 0)
        bound = qnorm * jnp.where(row < tq, kmax_ref[:, 0:1], kmax_ref[:, 1:2])
        consume(qi, 0, "diag", jnp.maximum(qi - 1, 0))
        slack_ref[...] = jnp.max(bound - m_ref[...], axis=0, keepdims=True)

    slack = jnp.max(slack_ref[...])

    def needed(rel):
        return slack + slope * rel >= -ZERO_PROB_LOG2

    def cond(carry):
        j, rel = carry
        return (j >= 2) & needed(rel - tk)

    def body(carry):
        j, rel = carry
        consume(j, 1, "plain", j - 1)
        consume(j - 1, 0, "plain", j - 2)
        return j - 2, rel - 2 * tk

    j_mid, rel_mid = lax.while_loop(cond, body, (qi - 1, jnp.float32(-1.0)))
    single = (j_mid >= 1) & needed(rel_mid)

    @pl.when(single)
    def _():
        consume(j_mid, 1, "plain", j_mid - 1)

    j_end = jnp.where(single, j_mid - 1, j_mid)
    rel_end = jnp.where(single, rel_mid - tk, rel_mid)
    for static_slot, in_this_slot in ((0, single), (1, jnp.logical_not(single))):
        @pl.when((j_end == 0) & needed(rel_end) & in_this_slot)
        def _():
            consume(0, static_slot, "first", None)

    o = acc_ref[...] / l_ref[...]
    lam = (jnp.exp(jnp.sum(lq1_ref[...] * lk1_ref[...], axis=1, keepdims=True))
           - jnp.exp(jnp.sum(lq2_ref[...] * lk2_ref[...], axis=1, keepdims=True)) + lambda_init)
    o = o[0:tq, :] - lam * o[tq:, :]
    o_ref[0] = (_rms(o, gn_ref[...]) * (1.0 - lambda_init)).astype(BF16)


def _diff_attention(proj, lam_q1, lam_k1, lam_q2, lam_k2, gn, lambda_init, col0):
    b, p, _ = proj.shape
    nh = N_DIFF_HEADS
    tq = tk = _pick_tile(p, (640, 512, 256, 128))
    rsub = tq
    slopes = jnp.asarray(2.0 ** (-8.0 * (np.arange(nh) + 1.0) / nh), F32)
    lam_spec = _const_spec((1, DIFF_QK_DIM))
    kv_block = (1, p, HEAD_W)
    sel = np.zeros((HEAD_W, HEAD_W), np.float32)
    sel[:DIFF_QK_DIM, 0] = 1.0
    sel[DIFF_QK_DIM:, 1] = 1.0
    return pl.pallas_call(
        functools.partial(_diff_attn_kernel, tq=tq, tk=tk, rsub=rsub, lambda_init=lambda_init),
        grid=(b, nh, p // tq),
        in_specs=[pl.BlockSpec(memory_space=pltpu.SMEM),
                  lam_spec, lam_spec, lam_spec, lam_spec,
                  pl.BlockSpec((1, tq, HEAD_W), lambda bi, h, i: (bi, i, col0 + h)),
                  pl.BlockSpec(kv_block, lambda bi, h, i: (bi, 0, col0 + nh + h)),
                  pl.BlockSpec(kv_block, lambda bi, h, i: (bi, 0, col0 + 2 * nh + h)),
                  _const_spec((HEAD_W, HEAD_W)),
                  pl.BlockSpec((1, HEAD_W), lambda bi, h, i: (0, h))],
        out_specs=pl.BlockSpec((1, tq, HEAD_W), lambda bi, h, i: (bi, i, h)),
        out_shape=jax.ShapeDtypeStruct((b, p, nh * HEAD_W), BF16),
        scratch_shapes=[pltpu.VMEM((2 * tq, HEAD_W), BF16),
                        pltpu.VMEM((2 * tq, 1), F32),
                        pltpu.VMEM((2 * tq, 1), F32),
                        pltpu.VMEM((2 * tq, HEAD_W), F32),
                        pltpu.VMEM((1, HEAD_W), F32),
                        pltpu.VMEM((2 * tq, tk), F32),
                        pltpu.VMEM((2 * tq, tk), F32),
                        pltpu.VMEM((1, 1), F32)],
        compiler_params=_params(("parallel", "parallel", "arbitrary")),
    )(slopes, lam_q1.reshape(1, -1), lam_k1.reshape(1, -1), lam_q2.reshape(1, -1), lam_k2.reshape(1, -1),
      proj, proj, proj, jnp.asarray(sel, BF16), gn.reshape(1, -1))


def _cumsum_blocks(tk):
    blocks, start = [], 0
    while start < tk:
        width = 256 if tk - start >= 256 else tk - start
        blocks.append((start, width))
        start += width
    return tuple(blocks)


def _sb_attn_kernel(q_ref, k_ref, v_ref, u_ref, o_ref, run_ref, acc_ref, *, tq, heads):
    qi = pl.program_id(2)
    q0 = pl.multiple_of(qi * tq, BLOCK)

    def attend(pieces):
        jobs = [(hd,) + tuple(piece) for piece in pieces for hd in range(heads)]
        n = len(jobs)
        z, softplus, log_beta, mask, later, w = ([None] * n for _ in range(6))
        for i, (hd, row_lo, n_rows, kstart, width, masked, first) in enumerate(jobs):
            cols = slice(hd * HEAD_W, (hd + 1) * HEAD_W)
            z[i] = _dot_nt(q_ref[0, row_lo:row_lo + n_rows, cols], k_ref[0, pl.ds(kstart, width), cols])
        for i, (hd, row_lo, n_rows, kstart, width, masked, first) in enumerate(jobs):
            neg_abs = lax.bitcast_convert_type(
                lax.bitcast_convert_type(z[i], jnp.uint32) | jnp.uint32(0x80000000), F32)
            sp = jnp.maximum(z[i], 0.0) + jnp.log(1.0 + jnp.exp2(neg_abs)) * INV_LN2
            log_beta[i] = z[i] - sp
            if masked:
                kpos = kstart + lax.broadcasted_iota(jnp.int32, (1, width), 1)
                if masked == "pad":
                    mask[i] = jnp.broadcast_to(kpos >= N_PAD, sp.shape)
                else:
                    qpos = q0 + row_lo + lax.broadcasted_iota(jnp.int32, (n_rows, 1), 0)
                    mask[i] = (qpos > kpos) if masked == "causal" else (qpos > kpos) & (kpos >= N_PAD)
                sp = jnp.where(mask[i], sp, 0.0)
            softplus[i] = sp
        for i, (hd, row_lo, n_rows, kstart, width, masked, first) in enumerate(jobs):
            rows = slice(row_lo, row_lo + n_rows)
            run = jnp.zeros((n_rows, 1), F32) if first else run_ref[hd, rows, :]
            sp16 = softplus[i].astype(BF16)
            blocks = _cumsum_blocks(width)
            parts = [None] * len(blocks)
            for idx in reversed(range(len(blocks))):
                b0, bw = blocks[idx]
                parts[idx] = _dot(sp16[:, b0:b0 + bw], u_ref[0:bw, 0:bw]) + run
                run = run - jnp.sum(softplus[i][:, b0:b0 + bw], axis=1, keepdims=True)
            run_ref[hd, rows, :] = run
            later[i] = parts[0] if len(parts) == 1 else jnp.concatenate(parts, axis=1)
        for i, (hd, row_lo, n_rows, kstart, width, masked, first) in enumerate(jobs):
            w[i] = jnp.exp2(log_beta[i] + later[i])
            if masked:
                w[i] = jnp.where(mask[i], w[i], 0.0)
            w[i] = w[i].astype(BF16)
        for i, (hd, row_lo, n_rows, kstart, width, masked, first) in enumerate(jobs):
            rows = slice(row_lo, row_lo + n_rows)
            cols = slice(hd * HEAD_W, (hd + 1) * HEAD_W)
            contrib = _dot(w[i], v_ref[0, pl.ds(kstart, width), cols])
            if first:
                acc_ref[rows, cols] = contrib
            else:
                acc_ref[rows, cols] += contrib

    def before_tile(row_lo, n_rows, kstart, width, masked):
        return [(row_lo + r, BLOCK, kstart, width, masked, False) for r in range(0, n_rows, BLOCK)]

    top = min(SB_TOP_ROWS, tq)
    n_blk = tq // BLOCK

    @pl.when(qi == 0)
    def _():
        attend([(r * BLOCK, BLOCK, q0, (r + 1) * BLOCK, "both", True) for r in range(n_blk)])

    def own(blocks):
        diag = [(r * BLOCK, BLOCK, pl.multiple_of(q0 + r * BLOCK, BLOCK), BLOCK, "causal", True) for r in blocks]
        rest = [(r * BLOCK, BLOCK, q0, r * BLOCK, False, False) for r in blocks if r > 0]
        return diag + rest

    @pl.when(qi > 0)
    def _():
        attend(own(range(n_blk))
               + before_tile(0, top, pl.multiple_of(q0 - SB_KEY_CHUNK, BLOCK), SB_KEY_CHUNK, False))

    def walk(row_lo, n_rows, chunks_done):
        def more_needed():
            return jnp.max(run_ref[:, row_lo:row_lo + n_rows, :]) > -ZERO_PROB_LOG2

        def chunk(kstart, width, masked):
            attend(before_tile(row_lo, n_rows, pl.multiple_of(kstart, BLOCK), width, masked))

        def cond(carry):
            kstart, go = carry
            return (kstart >= BLOCK) & go

        def body(carry):
            kstart, _ = carry
            chunk(kstart, SB_KEY_CHUNK, False)
            return kstart - SB_KEY_CHUNK, more_needed()

        kstart, go = lax.while_loop(cond, body, (q0 - (chunks_done + 1) * SB_KEY_CHUNK, more_needed()))

        def cond_tail(carry):
            kstart, go = carry
            return (kstart >= 0) & go

        def body_tail(carry):
            kstart, _ = carry
            chunk(kstart, BLOCK, "pad")
            return kstart - BLOCK, more_needed()

        lax.while_loop(cond_tail, body_tail, (kstart + SB_KEY_CHUNK - BLOCK, go))

    walk(0, top, 1)
    if top < tq:
        @pl.when(jnp.max(run_ref[:, top:, :]) > -ZERO_PROB_LOG2)
        def _():
            walk(top, tq - top, 0)

    o_ref[0] = acc_ref[...].astype(BF16)


def _sb_attention(qkv):
    b, p, _ = qkv.shape
    nh = N_SB_HEADS
    tq = _pick_tile(p, (640, 512, 256, 128))
    umax = SB_KEY_CHUNK
    jj = np.arange(umax)
    u = jnp.asarray(-(jj[:, None] > jj[None, :]).astype(np.float32), BF16)
    heads = SB_HEADS_PER_STEP
    groups = nh // heads
    gw = heads * HEAD_W
    kv_block = (1, p, gw)
    return pl.pallas_call(
        functools.partial(_sb_attn_kernel, tq=tq, heads=heads),
        grid=(b, groups, p // tq),
        in_specs=[pl.BlockSpec((1, tq, gw), lambda bi, g, i: (bi, i, g)),
                  pl.BlockSpec(kv_block, lambda bi, g, i: (bi, 0, groups + g)),
                  pl.BlockSpec(kv_block, lambda bi, g, i: (bi, 0, 2 * groups + g)),
                  _const_spec((umax, umax))],
        out_specs=pl.BlockSpec((1, tq, gw), lambda bi, g, i: (bi, i, g)),
        out_shape=jax.ShapeDtypeStruct((b, p, nh * HEAD_W), BF16),
        scratch_shapes=[pltpu.VMEM((heads, tq, 1), F32), pltpu.VMEM((tq, gw), F32)],
        compiler_params=_params(("parallel", "parallel", "arbitrary")),
    )(qkv, qkv, qkv, u)


def kernel(x, meta_tokens, mix_norm, ffn_norm, ffn_up, ffn_conv, ffn_conv_b, ffn_down, ab_w_in, ab_ret_norm,
           ab_diff_norm, ab_lam_q1, ab_lam_k1, ab_lam_q2, ab_lam_k2, ab_w_out, c_w_in, c_w_out, final_norm):
    b, s, d = x.shape
    depth = mix_norm.shape[0]
    p = s + BLOCK
    t = b * p
    h = None
    for i in range(depth):
        if i % 2 == 0:
            e = i // 2
            lambda_init = 0.8 - 0.6 * math.exp(-0.3 * i)
            w_in = ab_w_in[e].astype(BF16)
            scales = (1.0, 1.0, 1.0, 1.0, DIFF_QK_DIM ** -0.5 * LOG2_E, 1.0, 1.0)
            if i == 0:
                proj, h = _embed_norm_proj(x, meta_tokens, mix_norm[i], w_in, scales, 512)
            else:
                proj = _norm_proj(h.reshape(t, d), mix_norm[i], w_in, scales, 512).reshape(b, p, -1)
            ret = _retention(proj, ab_ret_norm[e])
            dif = _diff_attention(proj, ab_lam_q1[e], ab_lam_k1[e], ab_lam_q2[e], ab_lam_k2[e],
                                  ab_diff_norm[e], lambda_init, 4 * N_RET_HEADS)
            acts, w_out = [ret, dif], ab_w_out[e].astype(BF16)
        else:
            o = i // 2
            w_in = c_w_in[o].astype(BF16)
            n_chunks = w_in.shape[1] // 512
            q_chunks = N_SB_HEADS * HEAD_W // 512
            scales = tuple(HEAD_W ** -0.5 * LOG2_E if c < q_chunks else 1.0 for c in range(n_chunks))
            qkv = _norm_proj(h.reshape(t, d), mix_norm[i], w_in, scales, 512).reshape(b, p, -1)
            acts, w_out = [_sb_attention(qkv)], c_w_out[o].astype(BF16)
        h = _mixer_out_ffn(h, acts, w_out, ffn_norm[i], ffn_up[i].astype(BF16), ffn_conv[i], ffn_conv_b[i],
                           ffn_down[i].astype(BF16), final_norm if i == depth - 1 else None)
    return h
```
